```python
import math
import jax
import jax.numpy as jnp
from jax import lax
import numpy as np

D_MODEL = 4096
BATCH = 4
SEQ = 2048
DEPTH = 2
DEC_BATCH = 8
DEC_SEQ = 4
PAST_LEN = 16384
PAGE_SIZE = 128

D_MIX = D_MODEL
MEM_HEADS = 4
MEM_HEAD_DIM = D_MIX // (4 * MEM_HEADS)
D_MEM = MEM_HEADS * MEM_HEAD_DIM
D_TOK = D_MIX - D_MEM
N_MEM = 256
A_HEAD_DIM = 128
A_HEADS = D_TOK // A_HEAD_DIM
MOBA_BLOCK = 256
MOBA_TOPK = 3
QUERY_CHUNK = 16
ROPE_THETA = 10000.0
B_HEAD_DIM = 64
B_HEADS = D_TOK // B_HEAD_DIM
DECAY_LORA = 96
ICLR_LORA = 96
GATE_LORA = 384
B_SHIFT_COLS = 3 * D_TOK + DECAY_LORA + ICLR_LORA + GATE_LORA
LNX_EPS = 64e-5
D_FF = 256 * ((8 * D_MODEL // 3 + 255) // 256)
RMS_EPS = 1e-6
NEG_INF = -1e30
N_A_LAYERS = (DEPTH + 1) // 2
N_B_LAYERS = DEPTH // 2
W_IN_A_COLS = 3 * D_TOK + D_MEM
W_IN_B_COLS = B_SHIFT_COLS + D_MEM

kernel_name = "moba_rwkv7_macaron_memory_decoder_step"


def rms_norm(x, g, eps=RMS_EPS):
    xf = x.astype(jnp.float32)
    y = xf * lax.rsqrt(jnp.mean(xf * xf, axis=-1, keepdims=True) + eps)
    return (y * g.astype(jnp.float32)).astype(x.dtype)


def swiglu(x, w_gate, w_up, w_down):
    return (jax.nn.silu(x @ w_gate) * (x @ w_up)) @ w_down


def rope(x, pos):
    half = x.shape[-1] // 2
    inv_freq = jnp.power(jnp.float32(ROPE_THETA), -jnp.arange(half, dtype=jnp.float32) / half)
    ang = pos.astype(jnp.float32)[:, None] * inv_freq[None, :]
    cos, sin = jnp.cos(ang)[:, None, :], jnp.sin(ang)[:, None, :]
    xf = x.astype(jnp.float32)
    x1, x2 = xf[..., :half], xf[..., half:]
    return jnp.concatenate([x1 * cos - x2 * sin, x2 * cos + x1 * sin], axis=-1).astype(x.dtype)


def memory_kv(mem, g_norm, w_k, w_v, g_k):
    bn, m_len, _ = mem.shape
    m = rms_norm(mem, g_norm)
    k = rms_norm((m @ w_k).reshape(bn, m_len, MEM_HEADS, MEM_HEAD_DIM), g_k)
    v = (m @ w_v).reshape(bn, m_len, MEM_HEADS, MEM_HEAD_DIM)
    return k, v


def memory_attend(mq, g_q, mem_k, mem_v):
    bn, t, _ = mq.shape
    q = rms_norm(mq.reshape(bn, t, MEM_HEADS, MEM_HEAD_DIM), g_q).astype(jnp.float32)
    s = jnp.einsum("bthd,bmhd->bhtm", q, mem_k.astype(jnp.float32)) * (MEM_HEAD_DIM ** -0.5)
    p = jax.nn.softmax(s, axis=-1)
    o = jnp.einsum("bhtm,bmhd->bthd", p, mem_v.astype(jnp.float32))
    return o.reshape(bn, t, D_MEM).astype(mq.dtype)


def moba_qkv(proj, pos, g_q, g_k):
    bn, t, _ = proj.shape
    q, k, v, mq = jnp.split(proj, [D_TOK, 2 * D_TOK, 3 * D_TOK], axis=-1)
    heads = lambda z: z.reshape(bn, t, A_HEADS, A_HEAD_DIM)
    q = rope(rms_norm(heads(q), g_q), pos)
    k = rope(rms_norm(heads(k), g_k), pos)
    return q, k, heads(v), mq


def moba_combine(q, k_sel, v_sel, sel_ok, k_own, v_own, own_ok):
    qf = q.astype(jnp.float32) * (q.shape[-1] ** -0.5)
    s_sel = jnp.einsum("bhqd,bhqld->bhql", qf, k_sel.astype(jnp.float32))
    s_own = jnp.einsum("bhqd,bhmd->bhqm", qf, k_own.astype(jnp.float32))
    s = jnp.concatenate([jnp.where(sel_ok, s_sel, NEG_INF), jnp.where(own_ok, s_own, NEG_INF)], axis=-1)
    p = jax.nn.softmax(s, axis=-1)
    n_sel = k_sel.shape[3]
    return (jnp.einsum("bhql,bhqld->bhqd", p[..., :n_sel], v_sel.astype(jnp.float32))
            + jnp.einsum("bhqm,bhmd->bhqd", p[..., n_sel:], v_own.astype(jnp.float32)))


def moba_prompt(q, k, v):
    bn, s_len, h, dh = q.shape
    nb = -(-s_len // MOBA_BLOCK)
    pad = nb * MOBA_BLOCK - s_len
    qh = q.transpose(0, 2, 1, 3)
    kh = jnp.pad(k.transpose(0, 2, 1, 3), ((0, 0), (0, 0), (0, pad), (0, 0)))
    vh = jnp.pad(v.transpose(0, 2, 1, 3), ((0, 0), (0, 0), (0, pad), (0, 0)))
    kblk = kh.reshape(bn, h, nb, MOBA_BLOCK, dh)
    vblk = vh.reshape(bn, h, nb, MOBA_BLOCK, dh)
    n_sel = min(MOBA_TOPK, nb - 1)
    q_blk = jnp.arange(s_len) // MOBA_BLOCK
    if n_sel > 0:
        k_mean = jnp.mean(kblk.astype(jnp.float32), axis=3)
        gate = jnp.einsum("bhsd,bhnd->bhsn", qh.astype(jnp.float32), k_mean)
        fully_past = jnp.arange(nb)[None, :] < q_blk[:, None]
        gate = jnp.where(fully_past, gate, NEG_INF)
        _, sel = lax.top_k(gate, n_sel)
        sel_ok = sel < q_blk[:, None]
    else:
        sel = jnp.zeros((bn, h, s_len, 0), jnp.int32)
        sel_ok = jnp.zeros((bn, h, s_len, 0), bool)
    take = jax.vmap(jax.vmap(lambda blocks, ix: blocks[ix]))

    def chunk(c):
        q0 = c * QUERY_CHUNK
        qc = lax.dynamic_slice_in_dim(qh, q0, QUERY_CHUNK, axis=2)
        ic = lax.dynamic_slice_in_dim(sel, q0, QUERY_CHUNK, axis=2)
        oc = lax.dynamic_slice_in_dim(sel_ok, q0, QUERY_CHUNK, axis=2)
        ks = take(kblk, ic).reshape(bn, h, QUERY_CHUNK, n_sel * MOBA_BLOCK, dh)
        vs = take(vblk, ic).reshape(bn, h, QUERY_CHUNK, n_sel * MOBA_BLOCK, dh)
        ks_ok = jnp.repeat(oc, MOBA_BLOCK, axis=-1)
        b0 = (q0 // MOBA_BLOCK) * MOBA_BLOCK
        ko = lax.dynamic_slice_in_dim(kh, b0, MOBA_BLOCK, axis=2)
        vo = lax.dynamic_slice_in_dim(vh, b0, MOBA_BLOCK, axis=2)
        own_ok = (b0 + jnp.arange(MOBA_BLOCK))[None, :] <= (q0 + jnp.arange(QUERY_CHUNK))[:, None]
        return moba_combine(qc, ks, vs, ks_ok, ko, vo, own_ok)

    out = lax.map(chunk, jnp.arange(s_len // QUERY_CHUNK))
    return out.transpose(1, 0, 3, 2, 4).reshape(bn, s_len, h * dh).astype(q.dtype)


def moba_sample(q, k, v, k_pool, v_pool, page_table):
    bn, t, h, dh = q.shape
    n_full = PAST_LEN // MOBA_BLOCK
    n_sel = min(MOBA_TOPK, n_full)
    qh = q.transpose(0, 2, 1, 3)
    if n_sel > 0:
        pages_per_block = MOBA_BLOCK // PAGE_SIZE
        past_rows = k_pool[page_table[:, :n_full * pages_per_block]]
        k_mean = jnp.mean(past_rows.astype(jnp.float32).reshape(bn, n_full, MOBA_BLOCK, h, dh), axis=2)
        gate = jnp.einsum("bhtd,bnhd->bhtn", qh.astype(jnp.float32), k_mean)
        _, sel = lax.top_k(gate, n_sel)
        tok = sel[..., None] * MOBA_BLOCK + jnp.arange(MOBA_BLOCK)
        phys = page_table[jnp.arange(bn)[:, None, None, None, None], tok // PAGE_SIZE]
        off = tok % PAGE_SIZE
        head = jnp.arange(h)[None, :, None, None, None]
        ks = k_pool[phys, off, head].reshape(bn, h, t, n_sel * MOBA_BLOCK, dh)
        vs = v_pool[phys, off, head].reshape(bn, h, t, n_sel * MOBA_BLOCK, dh)
    else:
        ks = jnp.zeros((bn, h, t, 0, dh), k.dtype)
        vs = jnp.zeros((bn, h, t, 0, dh), v.dtype)
    ks_ok = jnp.ones((bn, h, t, n_sel * MOBA_BLOCK), bool)
    start = n_full * MOBA_BLOCK
    tail = start + jnp.arange(PAST_LEN - start)
    k_tail = k_pool[page_table[:, tail // PAGE_SIZE], tail % PAGE_SIZE]
    v_tail = v_pool[page_table[:, tail // PAGE_SIZE], tail % PAGE_SIZE]
    ko = jnp.concatenate([k_tail.astype(k.dtype), k], axis=1).transpose(0, 2, 1, 3)
    vo = jnp.concatenate([v_tail.astype(v.dtype), v], axis=1).transpose(0, 2, 1, 3)
    own_ok = (start + jnp.arange(ko.shape[2]))[None, :] <= (PAST_LEN + jnp.arange(t))[:, None]
    o = moba_combine(qh, ks, vs, ks_ok, ko, vo, own_ok)
    return o.transpose(0, 2, 1, 3).reshape(bn, t, h * dh).astype(q.dtype)


def rwkv7_recurrence(state, r, w, k, v, a, b):
    def step(s_mat, inp):
        r_t, w_t, k_t, v_t, a_t, b_t = inp
        sa = jnp.einsum("bhvk,bhk->bhv", s_mat, a_t)
        s_mat = (s_mat * w_t[:, :, None, :] + sa[..., None] * b_t[:, :, None, :]
                 + v_t[..., None] * k_t[:, :, None, :])
        return s_mat, jnp.einsum("bhvk,bhk->bhv", s_mat, r_t)
    xs = tuple(z.swapaxes(0, 1) for z in (r, w, k, v, a, b))
    state, ys = lax.scan(step, state, xs)
    return ys.swapaxes(0, 1), state


def rwkv7_time_mix(p, p_prev, state, mu, w0, w2, a0, a2, g2, k_k, k_a, r_k, lnx_w, lnx_b):
    bn, t, _ = p.shape
    f = lambda z: z.astype(jnp.float32)
    prev = jnp.concatenate([p_prev.astype(p.dtype), p[:, :-1]], axis=1)
    s = f(p + (prev - p) * mu)
    c3 = 3 * D_TOK
    r, k, v, xw, xa, xg = jnp.split(s, [D_TOK, 2 * D_TOK, c3, c3 + DECAY_LORA, c3 + DECAY_LORA + ICLR_LORA], axis=-1)
    log_w = -jax.nn.softplus(-(f(w0) + jnp.tanh(xw) @ f(w2))) - 0.5
    decay = jnp.exp(-jnp.exp(log_w))
    a = jax.nn.sigmoid(f(a0) + xa @ f(a2))
    g = jax.nn.sigmoid(xg) @ f(g2)
    heads = lambda z: z.reshape(bn, t, B_HEADS, B_HEAD_DIM)
    kk = heads(k * f(k_k))
    kk = kk / jnp.maximum(jnp.sqrt(jnp.sum(kk * kk, axis=-1, keepdims=True)), 1e-12)
    k = k * (1.0 + (a - 1.0) * f(k_a))
    rh, kh, vh, ah = heads(r), heads(k), heads(v), heads(a)
    y, state = rwkv7_recurrence(f(state), rh, heads(decay), kh, vh, -kk, kk * ah)
    mean = jnp.mean(y, axis=-1, keepdims=True)
    var = jnp.mean(jnp.square(y - mean), axis=-1, keepdims=True)
    y = ((y - mean) * lax.rsqrt(var + LNX_EPS)).reshape(bn, t, D_TOK) * f(lnx_w) + f(lnx_b)
    y = y + (jnp.sum(rh * kh * f(r_k), axis=-1, keepdims=True) * vh).reshape(bn, t, D_TOK)
    return (y * g).astype(p.dtype), state, p[:, -1]


def setup_inputs(seed: int = 0) -> dict:
    key = jax.random.key(seed)
    keys = iter(jax.random.split(key, 48))

    def normal(shape, scale=1.0):
        return jax.random.normal(next(keys), shape, jnp.float32) * scale

    def gain(shape):
        return 1.0 + 0.05 * jax.random.normal(next(keys), shape, jnp.float32)

    def uniform(shape, lo, hi):
        return jax.random.uniform(next(keys), shape, jnp.float32, minval=lo, maxval=hi)

    n_pages = PAST_LEN // PAGE_SIZE
    n_used = DEC_BATCH * n_pages
    n_pool = n_used + n_used // 4
    na, nb = N_A_LAYERS, N_B_LAYERS
    sd = D_MODEL ** -0.5
    return {
        "x_prompt": normal((BATCH, SEQ, D_MODEL)),
        "x_sample": normal((DEC_BATCH, DEC_SEQ, D_MODEL)),
        "mem_prompt": normal((BATCH, N_MEM, D_MODEL)),
        "cache_k": normal((na, n_pool, PAGE_SIZE, A_HEADS, A_HEAD_DIM)),
        "cache_v": normal((na, n_pool, PAGE_SIZE, A_HEADS, A_HEAD_DIM)),
        "page_table": jax.random.permutation(next(keys), n_pool)[:n_used].reshape(DEC_BATCH, n_pages).astype(jnp.int32),
        "cache_mem_k": normal((DEPTH, DEC_BATCH, N_MEM, MEM_HEADS, MEM_HEAD_DIM)),
        "cache_mem_v": normal((DEPTH, DEC_BATCH, N_MEM, MEM_HEADS, MEM_HEAD_DIM)),
        "state_wkv": normal((nb, DEC_BATCH, B_HEADS, B_HEAD_DIM, B_HEAD_DIM), 0.5),
        "state_shift": normal((nb, DEC_BATCH, B_SHIFT_COLS)),
        "ffn1_norm": gain((DEPTH, D_MODEL)),
        "ffn1_w_gate": normal((DEPTH, D_MODEL, D_FF), sd),
        "ffn1_w_up": normal((DEPTH, D_MODEL, D_FF), sd),
        "ffn1_w_down": normal((DEPTH, D_FF, D_MODEL), D_FF ** -0.5),
        "mix_norm": gain((DEPTH, D_MODEL)),
        "w_in_a": normal((na, D_MODEL, W_IN_A_COLS), sd),
        "g_q_a": gain((na, A_HEAD_DIM)),
        "g_k_a": gain((na, A_HEAD_DIM)),
        "w_in_b": normal((nb, D_MODEL, W_IN_B_COLS), sd),
        "mu_b": uniform((nb, B_SHIFT_COLS), 0.0, 1.0),
        "w0_b": uniform((nb, D_TOK), -4.0, 1.0),
        "w2_b": normal((nb, DECAY_LORA, D_TOK), 0.5 * DECAY_LORA ** -0.5),
        "a0_b": normal((nb, D_TOK), 0.1),
        "a2_b": normal((nb, ICLR_LORA, D_TOK), 0.5 * ICLR_LORA ** -0.5),
        "g2_b": normal((nb, GATE_LORA, D_TOK), GATE_LORA ** -0.5),
        "k_k_b": 0.85 + 0.05 * normal((nb, D_TOK)),
        "k_a_b": gain((nb, D_TOK)),
        "r_k_b": normal((nb, B_HEADS, B_HEAD_DIM), 0.1),
        "lnx_w_b": gain((nb, D_TOK)),
        "lnx_b_b": normal((nb, D_TOK), 0.01),
        "mem_norm": gain((DEPTH, D_MODEL)),
        "w_mem_k": normal((DEPTH, D_MODEL, D_MEM), sd),
        "w_mem_v": normal((DEPTH, D_MODEL, D_MEM), sd),
        "g_mem_q": gain((DEPTH, MEM_HEAD_DIM)),
        "g_mem_k": gain((DEPTH, MEM_HEAD_DIM)),
        "w_out": normal((DEPTH, D_MIX, D_MODEL), D_MIX ** -0.5),
        "ffn2_norm": gain((DEPTH, D_MODEL)),
        "ffn2_w_gate": normal((DEPTH, D_MODEL, D_FF), sd),
        "ffn2_w_up": normal((DEPTH, D_MODEL, D_FF), sd),
        "ffn2_w_down": normal((DEPTH, D_FF, D_MODEL), D_FF ** -0.5),
    }


def reference(x_prompt, x_sample, mem_prompt, cache_k, cache_v, page_table, cache_mem_k, cache_mem_v,
              state_wkv, state_shift, ffn1_norm, ffn1_w_gate, ffn1_w_up, ffn1_w_down, mix_norm,
              w_in_a, g_q_a, g_k_a, w_in_b, mu_b, w0_b, w2_b, a0_b, a2_b, g2_b, k_k_b, k_a_b, r_k_b,
              lnx_w_b, lnx_b_b, mem_norm, w_mem_k, w_mem_v, g_mem_q, g_mem_k, w_out,
              ffn2_norm, ffn2_w_gate, ffn2_w_up, ffn2_w_down):
    pos_p = jnp.arange(SEQ)
    pos_s = PAST_LEN + jnp.arange(DEC_SEQ)
    hp, hs = x_prompt, x_sample
    kp_l, vp_l, ks_l, vs_l = [], [], [], []
    wkvp_l, wkvs_l, shp_l, shs_l = [], [], [], []
    mkp_l, mvp_l = [], []
    for i in range(DEPTH):
        j = i // 2
        hp = hp + 0.5 * swiglu(rms_norm(hp, ffn1_norm[i]), ffn1_w_gate[i], ffn1_w_up[i], ffn1_w_down[i])
        hs = hs + 0.5 * swiglu(rms_norm(hs, ffn1_norm[i]), ffn1_w_gate[i], ffn1_w_up[i], ffn1_w_down[i])
        mem_kp, mem_vp = memory_kv(mem_prompt, mem_norm[i], w_mem_k[i], w_mem_v[i], g_mem_k[i])
        mkp_l.append(mem_kp)
        mvp_l.append(mem_vp)
        up = rms_norm(hp, mix_norm[i])
        us = rms_norm(hs, mix_norm[i])
        if i % 2 == 0:
            qp, kp, vp, mqp = moba_qkv(up @ w_in_a[j], pos_p, g_q_a[j], g_k_a[j])
            qs, ks, vs, mqs = moba_qkv(us @ w_in_a[j], pos_s, g_q_a[j], g_k_a[j])
            tok_p = moba_prompt(qp, kp, vp)
            tok_s = moba_sample(qs, ks, vs, cache_k[j], cache_v[j], page_table)
            kp_l.append(kp)
            vp_l.append(vp)
            ks_l.append(ks)
            vs_l.append(vs)
        else:
            proj_p = up @ w_in_b[j]
            proj_s = us @ w_in_b[j]
            mqp, mqs = proj_p[..., B_SHIFT_COLS:], proj_s[..., B_SHIFT_COLS:]
            tok_p, st_p, last_p = rwkv7_time_mix(
                proj_p[..., :B_SHIFT_COLS], jnp.zeros((BATCH, 1, B_SHIFT_COLS), proj_p.dtype),
                jnp.zeros((BATCH, B_HEADS, B_HEAD_DIM, B_HEAD_DIM), jnp.float32),
                mu_b[j], w0_b[j], w2_b[j], a0_b[j], a2_b[j], g2_b[j], k_k_b[j], k_a_b[j], r_k_b[j],
                lnx_w_b[j], lnx_b_b[j])
            tok_s, st_s, last_s = rwkv7_time_mix(
                proj_s[..., :B_SHIFT_COLS], state_shift[j][:, None, :], state_wkv[j],
                mu_b[j], w0_b[j], w2_b[j], a0_b[j], a2_b[j], g2_b[j], k_k_b[j], k_a_b[j], r_k_b[j],
                lnx_w_b[j], lnx_b_b[j])
            wkvp_l.append(st_p.astype(state_wkv.dtype))
            wkvs_l.append(st_s.astype(state_wkv.dtype))
            shp_l.append(last_p.astype(state_shift.dtype))
            shs_l.append(last_s.astype(state_shift.dtype))
        mem_p = memory_attend(mqp, g_mem_q[i], mem_kp, mem_vp)
        mem_s = memory_attend(mqs, g_mem_q[i], cache_mem_k[i], cache_mem_v[i])
        hp = hp + jnp.concatenate([tok_p, mem_p], axis=-1) @ w_out[i]
        hs = hs + jnp.concatenate([tok_s, mem_s], axis=-1) @ w_out[i]
        hp = hp + 0.5 * swiglu(rms_norm(hp, ffn2_norm[i]), ffn2_w_gate[i], ffn2_w_up[i], ffn2_w_down[i])
        hs = hs + 0.5 * swiglu(rms_norm(hs, ffn2_norm[i]), ffn2_w_gate[i], ffn2_w_up[i], ffn2_w_down[i])
    new_k_prompt = jnp.stack(kp_l)
    new_v_prompt = jnp.stack(vp_l)
    new_k_sample = jnp.stack(ks_l)
    new_v_sample = jnp.stack(vs_l)
    new_wkv_prompt = jnp.stack(wkvp_l)
    new_wkv_sample = jnp.stack(wkvs_l)
    new_shift_prompt = jnp.stack(shp_l)
    new_shift_sample = jnp.stack(shs_l)
    new_mem_k_prompt = jnp.stack(mkp_l)
    new_mem_v_prompt = jnp.stack(mvp_l)
    return (hp, hs, new_k_prompt, new_v_prompt, new_k_sample, new_v_sample, new_wkv_prompt, new_wkv_sample,
            new_shift_prompt, new_shift_sample, new_mem_k_prompt, new_mem_v_prompt)
```

```python
import functools
import math

import jax
import jax.numpy as jnp
from jax import lax
from jax.experimental import pallas as pl
from jax.experimental.pallas import tpu as pltpu

F32 = jnp.float32
BF16 = jnp.bfloat16
HIGHEST = lax.Precision.HIGHEST

D_MODEL = 4096
MEM_HEADS = 4
MEM_HEAD_DIM = 256
D_MEM = MEM_HEADS * MEM_HEAD_DIM
D_TOK = D_MODEL - D_MEM
N_MEM = 256
A_HEAD_DIM = 128
A_HEADS = D_TOK // A_HEAD_DIM
MOBA_BLOCK = 256
MOBA_TOPK = 3
PAGE_SIZE = 128
PAGES_PER_BLOCK = MOBA_BLOCK // PAGE_SIZE
ROPE_THETA = 10000.0
B_HEAD_DIM = 64
B_HEADS = D_TOK // B_HEAD_DIM
DECAY_LORA = 96
ICLR_LORA = 96
GATE_LORA = 384
LORA_COLS = DECAY_LORA + ICLR_LORA + GATE_LORA
B_SHIFT_COLS = 3 * D_TOK + LORA_COLS
LNX_EPS = 64e-5
RMS_EPS = 1e-6
NEG_INF = -1e30

V7X_VMEM_BYTES = 64 * 1024 * 1024
VMEM_LIMIT_BYTES = V7X_VMEM_BYTES - 8 * 1024 * 1024
LANES = 128
WKV_CHAINS_PER_GROUP = 64
WKV_V_SPLIT = LANES // WKV_CHAINS_PER_GROUP
WKV_V_LO = B_HEAD_DIM // WKV_V_SPLIT


def _params(n_grid):
    return pltpu.CompilerParams(
        dimension_semantics=("arbitrary",) * n_grid,
        vmem_limit_bytes=VMEM_LIMIT_BYTES)


def _rmsnorm_kernel(x_ref, g_ref, o_ref):
    x = x_ref[...]
    ms = jnp.mean(x * x, axis=-1, keepdims=True)
    o_ref[...] = (x * lax.rsqrt(ms + RMS_EPS) * g_ref[...]).astype(o_ref.dtype)


def rmsnorm_bf16(x, g):
    t, d = x.shape
    tr = min(t, 256)
    return pl.pallas_call(
        _rmsnorm_kernel,
        grid=(t // tr,),
        in_specs=[pl.BlockSpec((tr, d), lambda i: (i, 0)),
                  pl.BlockSpec((1, d), lambda i: (0, 0))],
        out_specs=pl.BlockSpec((tr, d), lambda i: (i, 0)),
        out_shape=jax.ShapeDtypeStruct((t, d), BF16),
        compiler_params=_params(1),
        name="rmsnorm",
    )(x, g.reshape(1, d))


def _swiglu_kernel(x_ref, wg_ref, wu_ref, o_ref, w_scr, *, tn):
    @pl.when(pl.program_id(1) == 0)
    def _():
        w_scr[:, :tn] = wg_ref[...].astype(BF16)
        w_scr[:, tn:] = wu_ref[...].astype(BF16)

    acc = jnp.dot(x_ref[...], w_scr[...], preferred_element_type=F32)
    gate = acc[:, :tn]
    up = acc[:, tn:]
    o_ref[...] = (gate * jax.nn.sigmoid(gate) * up).astype(o_ref.dtype)


def swiglu_up(x, w_gate, w_up, *, tm, tn):
    t, k = x.shape
    n = w_gate.shape[1]
    return pl.pallas_call(
        functools.partial(_swiglu_kernel, tn=tn),
        grid=(n // tn, t // tm),
        in_specs=[pl.BlockSpec((tm, k), lambda j, i: (i, 0)),
                  pl.BlockSpec((k, tn), lambda j, i: (0, j)),
                  pl.BlockSpec((k, tn), lambda j, i: (0, j))],
        out_specs=pl.BlockSpec((tm, tn), lambda j, i: (i, j)),
        out_shape=jax.ShapeDtypeStruct((t, n), BF16),
        scratch_shapes=[pltpu.VMEM((k, 2 * tn), BF16)],
        compiler_params=_params(2),
        name="swiglu_up",
    )(x, w_gate, w_up)


def _linear_kernel(*refs, n_pairs, has_res, res_scale, head_dim, has_rope):
    xs = refs[:n_pairs]
    ws = refs[n_pairs:2 * n_pairs]
    pos = 2 * n_pairs
    res_ref = gain_ref = cos_ref = sin_ref = None
    if has_res:
        res_ref = refs[pos]
        pos += 1
    if head_dim:
        gain_ref = refs[pos]
        pos += 1
    if has_rope:
        cos_ref, sin_ref = refs[pos], refs[pos + 1]
        pos += 2
    o_ref = refs[pos]
    w_scrs = refs[pos + 1:]

    @pl.when(pl.program_id(1) == 0)
    def _():
        for w_ref, w_scr in zip(ws, w_scrs):
            w_scr[...] = w_ref[...].astype(BF16)

    acc = jnp.dot(xs[0][...], w_scrs[0][...], preferred_element_type=F32)
    for x_ref, w_scr in zip(xs[1:], w_scrs[1:]):
        acc = acc + jnp.dot(x_ref[...], w_scr[...], preferred_element_type=F32)

    if has_res:
        o_ref[...] = (res_ref[...] + res_scale * acc).astype(o_ref.dtype)
    elif head_dim:
        for c in range(acc.shape[1] // head_dim):
            xh = acc[:, c * head_dim:(c + 1) * head_dim]
            ms = jnp.mean(xh * xh, axis=-1, keepdims=True)
            yh = xh * lax.rsqrt(ms + RMS_EPS) * gain_ref[...]
            if has_rope:
                yh = yh * cos_ref[...] + pltpu.roll(yh, head_dim // 2, 1) * sin_ref[...]
            o_ref[:, c * head_dim:(c + 1) * head_dim] = yh.astype(o_ref.dtype)
    else:
        o_ref[...] = acc.astype(o_ref.dtype)


def linear(pairs, *, n_cols, tm, tn, out_dtype=F32, res=None, res_scale=1.0,
           head_gain=None, rope=None, name="linear"):
    t = pairs[0][0].shape[0]
    in_specs, args, scratch = [], [], []
    for (x, xkb, w, wkb, wc0, kb) in pairs:
        in_specs.append(pl.BlockSpec((tm, kb), lambda j, i, xkb=xkb: (i, xkb)))
        args.append(x)
    for (x, xkb, w, wkb, wc0, kb) in pairs:
        in_specs.append(pl.BlockSpec((kb, tn), lambda j, i, wkb=wkb, wc0=wc0: (wkb, wc0 + j)))
        args.append(w)
        scratch.append(pltpu.VMEM((kb, tn), BF16))
    if res is not None:
        in_specs.append(pl.BlockSpec((tm, tn), lambda j, i: (i, j)))
        args.append(res)
    head_dim = 0
    if head_gain is not None:
        head_dim = head_gain.shape[0]
        in_specs.append(pl.BlockSpec((1, head_dim), lambda j, i: (0, 0)))
        args.append(head_gain.reshape(1, head_dim))
    if rope is not None:
        n_rope_blocks = rope[0].shape[0] // tm
        for tab in rope:
            in_specs.append(pl.BlockSpec((tm, head_dim), lambda j, i: (i % n_rope_blocks, 0)))
            args.append(tab)
    kernel = functools.partial(
        _linear_kernel, n_pairs=len(pairs), has_res=res is not None,
        res_scale=res_scale, head_dim=head_dim, has_rope=rope is not None)
    return pl.pallas_call(
        kernel,
        grid=(n_cols // tn, t // tm),
        in_specs=in_specs,
        out_specs=pl.BlockSpec((tm, tn), lambda j, i: (i, j)),
        out_shape=jax.ShapeDtypeStruct((t, n_cols), out_dtype),
        scratch_shapes=scratch,
        compiler_params=_params(2),
        name=name,
    )(*args)


def ffn_half_step(h, g_norm, w_gate, w_up, w_down, *, tm_up, tm_down):
    d_ff = w_gate.shape[1]
    x = rmsnorm_bf16(h, g_norm)
    act = swiglu_up(x, w_gate, w_up, tm=tm_up, tn=256)
    kb = d_ff // 2
    for half in range(2):
        h = linear([(act, half, w_down, half, 0, kb)], n_cols=D_MODEL, tm=tm_down,
                   tn=512, res=h, res_scale=0.5, name="ffn_down")
    return h


def _moba_prompt_kernel(q_ref, k_ref, v_ref, o_ref, sel_scr, *, n_blocks):
    qi = pl.program_id(2)
    blk = MOBA_BLOCK
    dn_t = (((1,), (1,)), ((), ()))
    dn_tn = (((0,), (0,)), ((), ()))
    q = q_ref[...]

    k_mean = jnp.mean(k_ref[...].reshape(n_blocks, blk, A_HEAD_DIM), axis=1)
    gate = lax.dot_general(k_mean, q, dn_t, precision=HIGHEST,
                           preferred_element_type=F32)
    blk_idx = lax.broadcasted_iota(jnp.int32, gate.shape, 0)
    fully_past = blk_idx < qi
    gate = jnp.where(fully_past, gate, NEG_INF)
    rank = jnp.zeros(gate.shape, F32)
    for m in range(n_blocks):
        gm = gate[m:m + 1, :]
        beats = jnp.where(gm > gate, 1.0, jnp.where((gm == gate) & (blk_idx > m), 1.0, 0.0))
        rank = rank + beats
    n_sel = min(MOBA_TOPK, n_blocks - 1)
    sel_scr[...] = jnp.where(fully_past & (rank < n_sel), 1.0, 0.0)

    qs = (q * (A_HEAD_DIM ** -0.5)).astype(BF16)

    def block_scores(start):
        kb = k_ref[pl.ds(start, blk), :].astype(BF16)
        vb = v_ref[pl.ds(start, blk), :].astype(BF16)
        return lax.dot_general(kb, qs, dn_t, preferred_element_type=F32), vb

    s, vb = block_scores(pl.multiple_of(qi * blk, blk))
    key_idx = lax.broadcasted_iota(jnp.int32, s.shape, 0)
    qry_idx = lax.broadcasted_iota(jnp.int32, s.shape, 1)
    s = jnp.where(key_idx <= qry_idx, s, NEG_INF)
    m0 = jnp.max(s, axis=0, keepdims=True)
    p = jnp.exp(s - m0)
    l0 = jnp.sum(p, axis=0, keepdims=True)
    acc0 = lax.dot_general(vb, p.astype(BF16), dn_tn, preferred_element_type=F32)

    def past_block(n, carry):
        m_run, l_run, acc = carry
        s, vb = block_scores(pl.multiple_of(n * blk, blk))
        s = jnp.where(sel_scr[pl.ds(n, 1), :] > 0.5, s, NEG_INF)
        m_new = jnp.maximum(m_run, jnp.max(s, axis=0, keepdims=True))
        alpha = jnp.exp(m_run - m_new)
        p = jnp.exp(s - m_new)
        l_new = l_run * alpha + jnp.sum(p, axis=0, keepdims=True)
        acc = acc * alpha + lax.dot_general(vb, p.astype(BF16), dn_tn,
                                            preferred_element_type=F32)
        return m_new, l_new, acc

    _, l_fin, acc = lax.fori_loop(0, qi, past_block, (m0, l0, acc0))
    o_ref[...] = (acc / l_fin).T.astype(o_ref.dtype)


def moba_prompt(q, k, v, *, batch, seq):
    n_blocks = seq // MOBA_BLOCK
    return pl.pallas_call(
        functools.partial(_moba_prompt_kernel, n_blocks=n_blocks),
        grid=(batch, A_HEADS, n_blocks),
        in_specs=[pl.BlockSpec((MOBA_BLOCK, A_HEAD_DIM), lambda b, h, i: (b * n_blocks + i, h)),
                  pl.BlockSpec((seq, A_HEAD_DIM), lambda b, h, i: (b, h)),
                  pl.BlockSpec((seq, A_HEAD_DIM), lambda b, h, i: (b, h))],
        out_specs=pl.BlockSpec((MOBA_BLOCK, A_HEAD_DIM), lambda b, h, i: (b * n_blocks + i, h)),
        out_shape=jax.ShapeDtypeStruct((batch * seq, D_TOK), BF16),
        scratch_shapes=[pltpu.VMEM((n_blocks, MOBA_BLOCK), F32)],
        compiler_params=_params(3),
        name="moba_prompt",
    )(q, k, v)


def _kmean_kernel(pt_ref, p0_ref, p1_ref, o_ref):
    del pt_ref
    s = jnp.sum(p0_ref[0], axis=0, keepdims=True) + jnp.sum(p1_ref[0], axis=0, keepdims=True)
    o_ref[0] = s * (1.0 / MOBA_BLOCK)


def cache_block_means(cache, page_table, *, n_full):
    db = page_table.shape[0]
    width = cache.shape[2]
    grid_spec = pltpu.PrefetchScalarGridSpec(
        num_scalar_prefetch=1,
        grid=(db, n_full),
        in_specs=[pl.BlockSpec((1, PAGE_SIZE, width), lambda b, n, pt: (pt[b, 2 * n], 0, 0)),
                  pl.BlockSpec((1, PAGE_SIZE, width), lambda b, n, pt: (pt[b, 2 * n + 1], 0, 0))],
        out_specs=pl.BlockSpec((1, 1, width), lambda b, n, pt: (b * n_full + n, 0, 0)),
    )
    out = pl.pallas_call(
        _kmean_kernel,
        grid_spec=grid_spec,
        out_shape=jax.ShapeDtypeStruct((db * n_full, 1, width), F32),
        compiler_params=_params(2),
        name="cache_block_means",
    )(page_table, cache, cache)
    return out.reshape(db, n_full, width)


def _sample_topk_kernel(q_ref, km_ref, o_ref, *, n_sel):
    dn_t = (((1,), (1,)), ((), ()))
    q = q_ref[0]
    km = km_ref[0]
    for h in range(A_HEADS):
        cols = slice(h * A_HEAD_DIM, (h + 1) * A_HEAD_DIM)
        gate = lax.dot_general(q[:, cols], km[:, cols], dn_t, precision=HIGHEST,
                               preferred_element_type=F32)
        blk = lax.broadcasted_iota(jnp.int32, gate.shape, 1).astype(F32)
        lane = lax.broadcasted_iota(jnp.int32, (gate.shape[0], LANES), 1)
        tile = jnp.zeros((gate.shape[0], LANES), F32)
        for r in range(n_sel):
            best = jnp.max(gate, axis=1, keepdims=True)
            idx = jnp.min(jnp.where(gate == best, blk, float(gate.shape[1])),
                          axis=1, keepdims=True)
            tile = jnp.where(lane == r, idx, tile)
            gate = jnp.where(blk == idx, -jnp.inf, gate)
        o_ref[0, h] = tile.astype(jnp.int32)


def sample_topk(q, k_mean, *, n_sel):
    db, t, width = q.shape
    n_full = k_mean.shape[1]
    out = pl.pallas_call(
        functools.partial(_sample_topk_kernel, n_sel=n_sel),
        grid=(db,),
        in_specs=[pl.BlockSpec((1, t, width), lambda b: (b, 0, 0)),
                  pl.BlockSpec((1, n_full, width), lambda b: (b, 0, 0))],
        out_specs=pl.BlockSpec((1, A_HEADS, t, LANES), lambda b: (b, 0, 0, 0)),
        out_shape=jax.ShapeDtypeStruct((db, A_HEADS, t, LANES), jnp.int32),
        compiler_params=_params(1),
        name="sample_topk",
    )(q, k_mean)
    return out[..., :n_sel]


def _moba_sample_kernel(ph_ref, q_ref, kn_ref, vn_ref, *refs, t_len, n_pages):
    del ph_ref
    k_pages = refs[:t_len * n_pages]
    v_pages = refs[t_len * n_pages:2 * t_len * n_pages]
    o_ref = refs[2 * t_len * n_pages]
    dn_t = (((1,), (1,)), ((), ()))
    q = q_ref[0] * (A_HEAD_DIM ** -0.5)
    kn = kn_ref[0]
    vn = vn_ref[0]
    own_idx = lax.broadcasted_iota(jnp.int32, (t_len, 1), 0)
    for t in range(t_len):
        qt = q[t:t + 1, :]
        s_pages = [lax.dot_general(qt, k_pages[t * n_pages + i][0], dn_t, precision=HIGHEST,
                                   preferred_element_type=F32) for i in range(n_pages)]
        s_own = jnp.sum(kn * qt, axis=1, keepdims=True)
        s_own = jnp.where(own_idx <= t, s_own, NEG_INF)
        m_row = s_pages[0]
        for s in s_pages[1:]:
            m_row = jnp.maximum(m_row, s)
        m = jnp.maximum(jnp.max(m_row, axis=1, keepdims=True),
                        jnp.max(s_own, axis=0, keepdims=True))
        p_own = jnp.exp(s_own - m)
        denom = jnp.sum(p_own, axis=0, keepdims=True)
        out = jnp.sum(p_own * vn, axis=0, keepdims=True)
        for i in range(n_pages):
            p = jnp.exp(s_pages[i] - m)
            denom = denom + jnp.sum(p, axis=1, keepdims=True)
            out = out + jnp.dot(p, v_pages[t * n_pages + i][0], precision=HIGHEST,
                                preferred_element_type=F32)
        o_ref[0, t:t + 1, :] = out / denom


def moba_sample(q, k_new, v_new, cache_k, cache_v, phys):
    db, t_len, width = q.shape
    n_pages = phys.shape[0] // (db * A_HEADS * t_len)
    per_step = t_len * n_pages

    def page_spec(i):
        return pl.BlockSpec(
            (1, PAGE_SIZE, A_HEAD_DIM),
            lambda b, h, ph, i=i: (ph[(b * A_HEADS + h) * per_step + i], 0, h))

    row_spec = pl.BlockSpec((1, t_len, A_HEAD_DIM), lambda b, h, ph: (b, 0, h))
    grid_spec = pltpu.PrefetchScalarGridSpec(
        num_scalar_prefetch=1,
        grid=(db, A_HEADS),
        in_specs=[row_spec, row_spec, row_spec]
        + [page_spec(i) for i in range(per_step)] * 2,
        out_specs=row_spec,
    )
    return pl.pallas_call(
        functools.partial(_moba_sample_kernel, t_len=t_len, n_pages=n_pages),
        grid_spec=grid_spec,
        out_shape=jax.ShapeDtypeStruct((db, t_len, width), F32),
        compiler_params=_params(2),
        name="moba_sample",
    )(phys, q, k_new, v_new, *([cache_k] * per_step), *([cache_v] * per_step))


def _mem_attend_kernel(mq_ref, k_ref, v_ref, g_ref, o_ref):
    dn_t = (((1,), (1,)), ((), ()))
    hd = MEM_HEAD_DIM
    for h in range(MEM_HEADS):
        cols = slice(h * hd, (h + 1) * hd)
        x = mq_ref[0, :, cols]
        ms = jnp.mean(x * x, axis=-1, keepdims=True)
        qn = (x * lax.rsqrt(ms + RMS_EPS) * g_ref[...]).astype(BF16)
        kh = k_ref[0, :, cols].astype(BF16)
        vh = v_ref[0, :, cols].astype(BF16)
        s = lax.dot_general(qn, kh, dn_t, preferred_element_type=F32) * (hd ** -0.5)
        s = s - jnp.max(s, axis=-1, keepdims=True)
        p = jnp.exp(s)
        denom = jnp.sum(p, axis=-1, keepdims=True)
        out = jnp.dot(p.astype(BF16), vh, preferred_element_type=F32) / denom
        o_ref[0, :, cols] = out.astype(o_ref.dtype)


def mem_attend(mq, g_q, mem_k, mem_v, *, tq, out_dtype):
    b, t, _ = mq.shape
    return pl.pallas_call(
        _mem_attend_kernel,
        grid=(b, t // tq),
        in_specs=[pl.BlockSpec((1, tq, D_MEM), lambda i, j: (i, j, 0)),
                  pl.BlockSpec((1, N_MEM, D_MEM), lambda i, j: (i, 0, 0)),
                  pl.BlockSpec((1, N_MEM, D_MEM), lambda i, j: (i, 0, 0)),
                  pl.BlockSpec((1, MEM_HEAD_DIM), lambda i, j: (0, 0))],
        out_specs=pl.BlockSpec((1, tq, D_MEM), lambda i, j: (i, j, 0)),
        out_shape=jax.ShapeDtypeStruct((b, t, D_MEM), out_dtype),
        compiler_params=_params(2),
        name="mem_attend",
    )(mq, mem_k, mem_v, g_q.reshape(1, MEM_HEAD_DIM))


def _head_ones(width):
    r = lax.broadcasted_iota(jnp.int32, (width, width), 0) // B_HEAD_DIM
    c = lax.broadcasted_iota(jnp.int32, (width, width), 1) // B_HEAD_DIM
    return jnp.where(r == c, 1.0, 0.0).astype(F32)


def _dot_f32(a, b):
    return jnp.dot(a, b, precision=HIGHEST, preferred_element_type=F32)


def _rwkv_prep_kernel(pr_ref, pk_ref, pv_ref, qr_ref, qk_ref, qv_ref,
                      xw_ref, xwp_ref, xa_ref, xap_ref,
                      mur_ref, muk_ref, muv_ref, muw_ref, mua_ref,
                      w0_ref, w2_ref, a0_ref, a2_ref, kk_ref, ka_ref,
                      r_out, w_out, k_out, v_out, a_out, b_out):
    def shift(cur_ref, prev_ref, mu_ref):
        cur = cur_ref[...]
        return cur + (prev_ref[...] - cur) * mu_ref[...]

    r = shift(pr_ref, qr_ref, mur_ref)
    k = shift(pk_ref, qk_ref, muk_ref)
    v = shift(pv_ref, qv_ref, muv_ref)
    xw = shift(xw_ref, xwp_ref, muw_ref)
    xa = shift(xa_ref, xap_ref, mua_ref)

    z = -(w0_ref[...] + _dot_f32(jnp.tanh(xw), w2_ref[...]))
    softplus = jnp.maximum(z, 0.0) + jnp.log(1.0 + jnp.exp(-jnp.abs(z)))
    decay = jnp.exp(-jnp.exp(-softplus - 0.5))
    a = jax.nn.sigmoid(a0_ref[...] + _dot_f32(xa, a2_ref[...]))

    kk = k * kk_ref[...]
    norm = jnp.sqrt(_dot_f32(kk * kk, _head_ones(kk.shape[1])))
    kk = kk / jnp.maximum(norm, 1e-12)

    r_out[...] = r
    w_out[...] = decay
    k_out[...] = k * (1.0 + (a - 1.0) * ka_ref[...])
    v_out[...] = v
    a_out[...] = -kk
    b_out[...] = kk * a


def rwkv_prep(p_rkv, prev_rkv, xw, xw_prev, xa, xa_prev, mu_rkv, mu_w, mu_a,
              w0, w2, a0, a2, k_k, k_a, *, tm, tc):
    t = p_rkv.shape[0]
    n_c = D_TOK // tc
    row = lambda v: v.reshape(1, -1)

    def sect(s):
        return pl.BlockSpec((tm, tc), lambda i, j, s=s: (i, s * n_c + j))

    def lora(width):
        return pl.BlockSpec((tm, width), lambda i, j: (i, 0))

    def vec(s=0):
        return pl.BlockSpec((1, tc), lambda i, j, s=s: (0, s * n_c + j))

    def full(r, c):
        return pl.BlockSpec((r, c), lambda i, j: (0, 0))

    in_specs = [sect(0), sect(1), sect(2), sect(0), sect(1), sect(2),
                lora(DECAY_LORA), lora(DECAY_LORA), lora(ICLR_LORA), lora(ICLR_LORA),
                vec(0), vec(1), vec(2), full(1, DECAY_LORA), full(1, ICLR_LORA),
                vec(), pl.BlockSpec((DECAY_LORA, tc), lambda i, j: (0, j)),
                vec(), pl.BlockSpec((ICLR_LORA, tc), lambda i, j: (0, j)),
                vec(), vec()]
    out_spec = pl.BlockSpec((tm, tc), lambda i, j: (i, j))
    return pl.pallas_call(
        _rwkv_prep_kernel,
        grid=(t // tm, n_c),
        in_specs=in_specs,
        out_specs=[out_spec] * 6,
        out_shape=[jax.ShapeDtypeStruct((t, D_TOK), F32)] * 6,
        compiler_params=_params(2),
        name="rwkv_prep",
    )(p_rkv, p_rkv, p_rkv, prev_rkv, prev_rkv, prev_rkv,
      xw, xw_prev, xa, xa_prev,
      row(mu_rkv), row(mu_rkv), row(mu_rkv), row(mu_w), row(mu_a),
      row(w0), w2, row(a0), a2, row(k_k), row(k_a))


def _wkv_kernel(a_ref, w_ref, b_ref, k_ref, r_ref, v_ref, s0_ref, y_ref, s_out_ref, s_scr, *, tc):
    n_k = B_HEAD_DIM
    n_acc = 4

    @pl.when(pl.program_id(1) == 0)
    def _():
        s_scr[...] = s0_ref[0]

    def token(t, carry):
        sa_parts = [None] * n_acc
        for k in range(n_k):
            term = s_scr[k] * a_ref[0, t, pl.ds(k, 1), :]
            sa_parts[k % n_acc] = term if sa_parts[k % n_acc] is None else sa_parts[k % n_acc] + term
        sa = (sa_parts[0] + sa_parts[1]) + (sa_parts[2] + sa_parts[3])
        vt = v_ref[0, t]
        y_parts = [None] * n_acc
        for k in range(n_k):
            row = pl.ds(k, 1)
            s_new = (s_scr[k] * w_ref[0, t, row, :] + sa * b_ref[0, t, row, :]
                     + vt * k_ref[0, t, row, :])
            s_scr[k] = s_new
            term = s_new * r_ref[0, t, row, :]
            y_parts[k % n_acc] = term if y_parts[k % n_acc] is None else y_parts[k % n_acc] + term
        y_ref[0, t] = (y_parts[0] + y_parts[1]) + (y_parts[2] + y_parts[3])
        return carry

    lax.fori_loop(0, tc, token, 0)

    @pl.when(pl.program_id(1) == pl.num_programs(1) - 1)
    def _():
        s_out_ref[0] = s_scr[...]


def wkv_recurrence(a, w, b, k, r, v, s0, *, tc):
    g, t = a.shape[0], a.shape[1]
    op_spec = pl.BlockSpec((1, tc, B_HEAD_DIM, LANES), lambda i, j: (i, j, 0, 0))
    v_spec = pl.BlockSpec((1, tc, WKV_V_LO, LANES), lambda i, j: (i, j, 0, 0))
    s_spec = pl.BlockSpec((1, B_HEAD_DIM, WKV_V_LO, LANES), lambda i, j: (i, 0, 0, 0))
    return pl.pallas_call(
        functools.partial(_wkv_kernel, tc=tc),
        grid=(g, t // tc),
        in_specs=[op_spec] * 5 + [v_spec, s_spec],
        out_specs=[v_spec, s_spec],
        out_shape=[jax.ShapeDtypeStruct((g, t, WKV_V_LO, LANES), F32),
                   jax.ShapeDtypeStruct((g, B_HEAD_DIM, WKV_V_LO, LANES), F32)],
        scratch_shapes=[pltpu.VMEM((B_HEAD_DIM, WKV_V_LO, LANES), F32)],
        compiler_params=_params(2),
        name="wkv_recurrence",
    )(a, w, b, k, r, v, s0)


def _rwkv_post_kernel(y_ref, r_ref, k_ref, v_ref, xg_ref, xgp_ref, mug_ref, g2_ref,
                      rk_ref, lw_ref, lb_ref, o_ref):
    y = y_ref[...]
    ones = _head_ones(y.shape[1])
    inv_n = 1.0 / B_HEAD_DIM
    mean = _dot_f32(y, ones) * inv_n
    d = y - mean
    var = _dot_f32(d * d, ones) * inv_n
    yn = d * lax.rsqrt(var + LNX_EPS) * lw_ref[...] + lb_ref[...]
    bonus = _dot_f32(r_ref[...] * k_ref[...] * rk_ref[...], ones) * v_ref[...]
    xg = xg_ref[...]
    xg = xg + (xgp_ref[...] - xg) * mug_ref[...]
    gate = _dot_f32(jax.nn.sigmoid(xg), g2_ref[...])
    o_ref[...] = ((yn + bonus) * gate).astype(o_ref.dtype)


def rwkv_post(y, r, k, v, xg, xg_prev, mu_g, g2, r_k, lnx_w, lnx_b, *, tm, tc, out_dtype):
    t = y.shape[0]
    row = lambda x: x.reshape(1, -1)
    tile = pl.BlockSpec((tm, tc), lambda i, j: (i, j))
    vec = pl.BlockSpec((1, tc), lambda i, j: (0, j))
    lora = pl.BlockSpec((tm, GATE_LORA), lambda i, j: (i, 0))
    return pl.pallas_call(
        _rwkv_post_kernel,
        grid=(t // tm, D_TOK // tc),
        in_specs=[tile, tile, tile, tile, lora, lora,
                  pl.BlockSpec((1, GATE_LORA), lambda i, j: (0, 0)),
                  pl.BlockSpec((GATE_LORA, tc), lambda i, j: (0, j)),
                  vec, vec, vec],
        out_specs=tile,
        out_shape=jax.ShapeDtypeStruct((t, D_TOK), out_dtype),
        compiler_params=_params(2),
        name="rwkv_post",
    )(y, r, k, v, xg, xg_prev, row(mu_g), g2, row(r_k), row(lnx_w), row(lnx_b))


def _to_wkv_lanes(x, batch, t_len):
    n_chain = batch * B_HEADS
    g = n_chain // WKV_CHAINS_PER_GROUP
    x = x.reshape(batch, t_len, B_HEADS, B_HEAD_DIM).transpose(1, 3, 0, 2)
    x = x.reshape(t_len, B_HEAD_DIM, g, WKV_CHAINS_PER_GROUP).transpose(2, 0, 1, 3)
    return jnp.concatenate([x] * WKV_V_SPLIT, axis=-1)


def _v_to_wkv_lanes(x, batch, t_len):
    n_chain = batch * B_HEADS
    g = n_chain // WKV_CHAINS_PER_GROUP
    x = x.reshape(batch, t_len, B_HEADS, WKV_V_SPLIT, WKV_V_LO).transpose(1, 4, 3, 0, 2)
    x = x.reshape(t_len, WKV_V_LO, WKV_V_SPLIT, g, WKV_CHAINS_PER_GROUP).transpose(3, 0, 1, 2, 4)
    return x.reshape(g, t_len, WKV_V_LO, LANES)


def _v_from_wkv_lanes(y, batch, t_len):
    g = y.shape[0]
    y = y.reshape(g, t_len, WKV_V_LO, WKV_V_SPLIT, WKV_CHAINS_PER_GROUP).transpose(1, 3, 2, 0, 4)
    y = y.reshape(t_len, WKV_V_SPLIT, WKV_V_LO, batch, B_HEADS).transpose(3, 0, 4, 1, 2)
    return y.reshape(batch * t_len, D_TOK)


def _state_to_wkv_lanes(s, batch):
    g = batch * B_HEADS // WKV_CHAINS_PER_GROUP
    s = s.reshape(batch, B_HEADS, WKV_V_SPLIT, WKV_V_LO, B_HEAD_DIM).transpose(4, 3, 2, 0, 1)
    s = s.reshape(B_HEAD_DIM, WKV_V_LO, WKV_V_SPLIT, g, WKV_CHAINS_PER_GROUP).transpose(3, 0, 1, 2, 4)
    return s.reshape(g, B_HEAD_DIM, WKV_V_LO, LANES)


def _state_from_wkv_lanes(s, batch):
    g = s.shape[0]
    s = s.reshape(g, B_HEAD_DIM, WKV_V_LO, WKV_V_SPLIT, WKV_CHAINS_PER_GROUP).transpose(1, 3, 2, 0, 4)
    s = s.reshape(B_HEAD_DIM, WKV_V_SPLIT, WKV_V_LO, batch, B_HEADS).transpose(3, 4, 1, 2, 0)
    return s.reshape(batch, B_HEADS, B_HEAD_DIM, B_HEAD_DIM)


def rwkv_time_mix(p_rkv, p_lora, shift_in, state_in, prm, *, batch, t_len, tm, wkv_tc, out_dtype):
    t = batch * t_len
    p_all = jnp.concatenate([p_rkv, p_lora], axis=1).reshape(batch, t_len, B_SHIFT_COLS)
    prev = jnp.concatenate([shift_in[:, None, :], p_all[:, :-1]], axis=1).reshape(t, B_SHIFT_COLS)
    c3 = 3 * D_TOK
    cw, ca = c3 + DECAY_LORA, c3 + DECAY_LORA + ICLR_LORA
    prev_rkv, xw_prev, xa_prev, xg_prev = prev[:, :c3], prev[:, c3:cw], prev[:, cw:ca], prev[:, ca:]
    xw, xa, xg = (p_lora[:, :DECAY_LORA], p_lora[:, DECAY_LORA:DECAY_LORA + ICLR_LORA],
                  p_lora[:, DECAY_LORA + ICLR_LORA:])
    mu = prm["mu"]
    r, w, k, v, a, b = rwkv_prep(
        p_rkv, prev_rkv, xw, xw_prev, xa, xa_prev, mu[:c3], mu[c3:cw], mu[cw:ca],
        prm["w0"], prm["w2"], prm["a0"], prm["a2"], prm["k_k"], prm["k_a"], tm=tm, tc=256)
    lanes = functools.partial(_to_wkv_lanes, batch=batch, t_len=t_len)
    y, s_out = wkv_recurrence(lanes(a), lanes(w), lanes(b), lanes(k), lanes(r),
                              _v_to_wkv_lanes(v, batch, t_len),
                              _state_to_wkv_lanes(state_in, batch), tc=wkv_tc)
    y = _v_from_wkv_lanes(y, batch, t_len)
    tok = rwkv_post(y, r, k, v, xg, xg_prev, mu[ca:], prm["g2"], prm["r_k"].reshape(-1),
                    prm["lnx_w"], prm["lnx_b"], tm=tm, tc=256, out_dtype=out_dtype)
    return tok, _state_from_wkv_lanes(s_out, batch), p_all[:, -1]


def _rope_tables(pos):
    half = A_HEAD_DIM // 2
    inv_freq = jnp.power(jnp.float32(ROPE_THETA), -jnp.arange(half, dtype=F32) / half)
    ang = pos.astype(F32)[:, None] * inv_freq[None, :]
    cos, sin = jnp.cos(ang), jnp.sin(ang)
    return jnp.concatenate([cos, cos], axis=1), jnp.concatenate([-sin, sin], axis=1)


def kernel(x_prompt, x_sample, mem_prompt, cache_k, cache_v, page_table, cache_mem_k, cache_mem_v, state_wkv, state_shift, ffn1_norm, ffn1_w_gate, ffn1_w_up, ffn1_w_down, mix_norm, w_in_a, g_q_a, g_k_a, w_in_b, mu_b, w0_b, w2_b, a0_b, a2_b, g2_b, k_k_b, k_a_b, r_k_b, lnx_w_b, lnx_b_b, mem_norm, w_mem_k, w_mem_v, g_mem_q, g_mem_k, w_out, ffn2_norm, ffn2_w_gate, ffn2_w_up, ffn2_w_down):
    batch, seq, _ = x_prompt.shape
    dec_batch, dec_seq, _ = x_sample.shape
    depth = ffn1_norm.shape[0]
    n_a_layers, n_pool = cache_k.shape[0], cache_k.shape[1]
    n_pages = page_table.shape[1]
    past_len = n_pages * PAGE_SIZE
    n_full = past_len // MOBA_BLOCK
    assert past_len % MOBA_BLOCK == 0
    n_sel_s = min(MOBA_TOPK, n_full)
    tp, ts = batch * seq, dec_batch * dec_seq
    tm_p = 1024

    hp = x_prompt.reshape(tp, D_MODEL)
    hs = x_sample.reshape(ts, D_MODEL)
    rope_p = _rope_tables(jnp.arange(seq))
    rope_s = _rope_tables(jnp.tile(past_len + jnp.arange(dec_seq), dec_batch))
    cache_k_flat = cache_k.reshape(n_a_layers * n_pool, PAGE_SIZE, D_TOK)
    cache_v_flat = cache_v.reshape(n_a_layers * n_pool, PAGE_SIZE, D_TOK)

    outs = {name: [] for name in ("kp", "vp", "ks", "vs", "wkvp", "wkvs", "shp", "shs", "mkp", "mvp")}
    for i in range(depth):
        j = i // 2
        hp = ffn_half_step(hp, ffn1_norm[i], ffn1_w_gate[i], ffn1_w_up[i], ffn1_w_down[i],
                           tm_up=tm_p, tm_down=512)
        hs = ffn_half_step(hs, ffn1_norm[i], ffn1_w_gate[i], ffn1_w_up[i], ffn1_w_down[i],
                           tm_up=ts, tm_down=ts)

        mem_x = rmsnorm_bf16(mem_prompt.reshape(batch * N_MEM, D_MODEL), mem_norm[i])
        mem_kp = linear([(mem_x, 0, w_mem_k[i], 0, 0, D_MODEL)], n_cols=D_MEM, tm=batch * N_MEM,
                        tn=512, head_gain=g_mem_k[i], name="mem_k")
        mem_vp = linear([(mem_x, 0, w_mem_v[i], 0, 0, D_MODEL)], n_cols=D_MEM, tm=batch * N_MEM,
                        tn=512, name="mem_v")
        outs["mkp"].append(mem_kp.reshape(batch, N_MEM, MEM_HEADS, MEM_HEAD_DIM))
        outs["mvp"].append(mem_vp.reshape(batch, N_MEM, MEM_HEADS, MEM_HEAD_DIM))

        up = rmsnorm_bf16(hp, mix_norm[i])
        us = rmsnorm_bf16(hs, mix_norm[i])
        if i % 2 == 0:
            w_in = w_in_a[j]
            n_tok_blocks = D_TOK // 512

            def project(u, tm, rope, w_in=w_in):
                q = linear([(u, 0, w_in, 0, 0, D_MODEL)], n_cols=D_TOK, tm=tm, tn=512,
                           head_gain=g_q_a[j], rope=rope, name="in_a_q")
                k = linear([(u, 0, w_in, 0, n_tok_blocks, D_MODEL)], n_cols=D_TOK, tm=tm, tn=512,
                           head_gain=g_k_a[j], rope=rope, name="in_a_k")
                v = linear([(u, 0, w_in, 0, 2 * n_tok_blocks, D_MODEL)], n_cols=D_TOK, tm=tm,
                           tn=512, name="in_a_v")
                mq = linear([(u, 0, w_in, 0, 3 * n_tok_blocks, D_MODEL)], n_cols=D_MEM, tm=tm,
                            tn=512, name="in_a_mq")
                return q, k, v, mq

            qp, kp, vp, mqp = project(up, tm_p, rope_p)
            qs, ks, vs, mqs = project(us, ts, rope_s)
            tok_p = moba_prompt(qp, kp, vp, batch=batch, seq=seq)

            qs3 = qs.reshape(dec_batch, dec_seq, D_TOK)
            page_ids = page_table + j * n_pool
            k_mean = cache_block_means(cache_k_flat, page_ids, n_full=n_full)
            sel = sample_topk(qs3, k_mean, n_sel=n_sel_s)
            logical = sel[..., None] * PAGES_PER_BLOCK + jnp.arange(PAGES_PER_BLOCK)
            phys = jnp.take_along_axis(
                page_ids[:, None, None, :],
                logical.reshape(dec_batch, A_HEADS, dec_seq, n_sel_s * PAGES_PER_BLOCK), axis=-1)
            tok_s = moba_sample(qs3, ks.reshape(dec_batch, dec_seq, D_TOK),
                                vs.reshape(dec_batch, dec_seq, D_TOK),
                                cache_k_flat, cache_v_flat, phys.reshape(-1).astype(jnp.int32))
            tok_s = tok_s.reshape(ts, D_TOK).astype(BF16)
            outs["kp"].append(kp.reshape(batch, seq, A_HEADS, A_HEAD_DIM))
            outs["vp"].append(vp.reshape(batch, seq, A_HEADS, A_HEAD_DIM))
            outs["ks"].append(ks.reshape(dec_batch, dec_seq, A_HEADS, A_HEAD_DIM))
            outs["vs"].append(vs.reshape(dec_batch, dec_seq, A_HEADS, A_HEAD_DIM))
        else:
            c3 = 3 * D_TOK
            w_in = w_in_b[j]
            w_lora = w_in[:, c3:B_SHIFT_COLS]
            w_mq = w_in[:, B_SHIFT_COLS:]
            prm = dict(mu=mu_b[j], w0=w0_b[j], w2=w2_b[j], a0=a0_b[j], a2=a2_b[j], g2=g2_b[j],
                       k_k=k_k_b[j], k_a=k_a_b[j], r_k=r_k_b[j], lnx_w=lnx_w_b[j], lnx_b=lnx_b_b[j])

            def project(u, tm, w_in=w_in, w_lora=w_lora, w_mq=w_mq):
                p_rkv = linear([(u, 0, w_in, 0, 0, D_MODEL)], n_cols=c3, tm=tm, tn=512, name="in_b_rkv")
                p_lora = linear([(u, 0, w_lora, 0, 0, D_MODEL)], n_cols=LORA_COLS, tm=tm,
                                tn=LORA_COLS, name="in_b_lora")
                mq = linear([(u, 0, w_mq, 0, 0, D_MODEL)], n_cols=D_MEM, tm=tm, tn=512, name="in_b_mq")
                return p_rkv, p_lora, mq

            rkv_p, lora_p, mqp = project(up, tm_p)
            rkv_s, lora_s, mqs = project(us, ts)
            tok_p, st_p, last_p = rwkv_time_mix(
                rkv_p, lora_p, jnp.zeros((batch, B_SHIFT_COLS), F32),
                jnp.zeros((batch, B_HEADS, B_HEAD_DIM, B_HEAD_DIM), F32), prm,
                batch=batch, t_len=seq, tm=256, wkv_tc=32, out_dtype=BF16)
            tok_s, st_s, last_s = rwkv_time_mix(
                rkv_s, lora_s, state_shift[j], state_wkv[j], prm,
                batch=dec_batch, t_len=dec_seq, tm=ts, wkv_tc=dec_seq, out_dtype=BF16)
            outs["wkvp"].append(st_p)
            outs["wkvs"].append(st_s)
            outs["shp"].append(last_p)
            outs["shs"].append(last_s)

        mem_p = mem_attend(mqp.reshape(batch, seq, D_MEM), g_mem_q[i],
                           mem_kp.reshape(batch, N_MEM, D_MEM), mem_vp.reshape(batch, N_MEM, D_MEM),
                           tq=512, out_dtype=BF16).reshape(tp, D_MEM)
        mem_s = mem_attend(mqs.reshape(dec_batch, dec_seq, D_MEM), g_mem_q[i],
                           cache_mem_k[i].reshape(dec_batch, N_MEM, D_MEM),
                           cache_mem_v[i].reshape(dec_batch, N_MEM, D_MEM),
                           tq=dec_seq, out_dtype=F32).reshape(ts, D_MEM).astype(BF16)
        mem_row_block = D_TOK // D_MEM
        hp = linear([(tok_p, 0, w_out[i], 0, 0, D_TOK), (mem_p, 0, w_out[i], mem_row_block, 0, D_MEM)],
                    n_cols=D_MODEL, tm=tm_p, tn=512, res=hp, name="out_proj")
        hs = linear([(tok_s, 0, w_out[i], 0, 0, D_TOK), (mem_s, 0, w_out[i], mem_row_block, 0, D_MEM)],
                    n_cols=D_MODEL, tm=ts, tn=512, res=hs, name="out_proj")

        hp = ffn_half_step(hp, ffn2_norm[i], ffn2_w_gate[i], ffn2_w_up[i], ffn2_w_down[i],
                           tm_up=tm_p, tm_down=512)
        hs = ffn_half_step(hs, ffn2_norm[i], ffn2_w_gate[i], ffn2_w_up[i], ffn2_w_down[i],
                           tm_up=ts, tm_down=ts)

    st = lambda name: jnp.stack(outs[name])
    return (hp.reshape(batch, seq, D_MODEL), hs.reshape(dec_batch, dec_seq, D_MODEL),
            st("kp"), st("vp"), st("ks"), st("vs"), st("wkvp"), st("wkvs"),
            st("shp"), st("shs"), st("mkp"), st("mvp"))
```

```python
import functools
import math

import jax
import jax.numpy as jnp
from jax import lax
from jax.experimental import pallas as pl
from jax.experimental.pallas import tpu as pltpu

F32 = jnp.float32
BF16 = jnp.bfloat16
HIGHEST = lax.Precision.HIGHEST

D_MODEL = 4096
MEM_HEADS = 4
MEM_HEAD_DIM = 256
D_MEM = MEM_HEADS * MEM_HEAD_DIM
D_TOK = D_MODEL - D_MEM
N_MEM = 256
A_HEAD_DIM = 128
A_HEADS = D_TOK // A_HEAD_DIM
MOBA_BLOCK = 256
MOBA_TOPK = 3
PAGE_SIZE = 128
PAGES_PER_BLOCK = MOBA_BLOCK // PAGE_SIZE
ROPE_THETA = 10000.0
B_HEAD_DIM = 64
B_HEADS = D_TOK // B_HEAD_DIM
DECAY_LORA = 96
ICLR_LORA = 96
GATE_LORA = 384
LORA_COLS = DECAY_LORA + ICLR_LORA + GATE_LORA
B_SHIFT_COLS = 3 * D_TOK + LORA_COLS
LNX_EPS = 64e-5
RMS_EPS = 1e-6
NEG_INF = -1e30

V7X_VMEM_BYTES = 64 * 1024 * 1024
VMEM_LIMIT_BYTES = V7X_VMEM_BYTES - 8 * 1024 * 1024
LANES = 128
SUBLANES = 8
MXU_WIDTH = 256
WKV_CHAINS_PER_GROUP = 64
WKV_V_SPLIT = LANES // WKV_CHAINS_PER_GROUP
WKV_V_LO = B_HEAD_DIM // WKV_V_SPLIT


def _params(n_grid):
    return pltpu.CompilerParams(
        dimension_semantics=("arbitrary",) * n_grid,
        vmem_limit_bytes=VMEM_LIMIT_BYTES)


def _rmsnorm_kernel(x_ref, g_ref, o_ref):
    x = x_ref[...]
    ms = jnp.mean(x * x, axis=-1, keepdims=True)
    o_ref[...] = (x * lax.rsqrt(ms + RMS_EPS) * g_ref[...]).astype(o_ref.dtype)


def rmsnorm_bf16(x, g):
    t, d = x.shape
    tr = min(t, 256)
    return pl.pallas_call(
        _rmsnorm_kernel,
        grid=(t // tr,),
        in_specs=[pl.BlockSpec((tr, d), lambda i: (i, 0)),
                  pl.BlockSpec((1, d), lambda i: (0, 0))],
        out_specs=pl.BlockSpec((tr, d), lambda i: (i, 0)),
        out_shape=jax.ShapeDtypeStruct((t, d), BF16),
        compiler_params=_params(1),
        name="rmsnorm",
    )(x, g.reshape(1, d))


def _swiglu_kernel(x_ref, wg_ref, wu_ref, o_ref, w_scr, *, tn):
    @pl.when(pl.program_id(1) == 0)
    def _():
        w_scr[:, :tn] = wg_ref[...].astype(BF16)
        w_scr[:, tn:] = wu_ref[...].astype(BF16)

    acc = jnp.dot(x_ref[...], w_scr[...], preferred_element_type=F32)
    gate = acc[:, :tn]
    up = acc[:, tn:]
    o_ref[...] = (gate * jax.nn.sigmoid(gate) * up).astype(o_ref.dtype)


def swiglu_up(x, w_gate, w_up, *, tm, tn):
    t, k = x.shape
    n = w_gate.shape[1]
    return pl.pallas_call(
        functools.partial(_swiglu_kernel, tn=tn),
        grid=(n // tn, t // tm),
        in_specs=[pl.BlockSpec((tm, k), lambda j, i: (i, 0)),
                  pl.BlockSpec((k, tn), lambda j, i: (0, j)),
                  pl.BlockSpec((k, tn), lambda j, i: (0, j))],
        out_specs=pl.BlockSpec((tm, tn), lambda j, i: (i, j)),
        out_shape=jax.ShapeDtypeStruct((t, n), BF16),
        scratch_shapes=[pltpu.VMEM((k, 2 * tn), BF16)],
        compiler_params=_params(2),
        name="swiglu_up",
    )(x, w_gate, w_up)


def _linear_kernel(*refs, n_pairs, has_res, res_scale, head_dim, has_rope):
    xs = refs[:n_pairs]
    ws = refs[n_pairs:2 * n_pairs]
    pos = 2 * n_pairs
    res_ref = gain_ref = cos_ref = sin_ref = None
    if has_res:
        res_ref = refs[pos]
        pos += 1
    if head_dim:
        gain_ref = refs[pos]
        pos += 1
    if has_rope:
        cos_ref, sin_ref = refs[pos], refs[pos + 1]
        pos += 2
    o_ref = refs[pos]
    w_scrs = refs[pos + 1:]

    @pl.when(pl.program_id(1) == 0)
    def _():
        for w_ref, w_scr in zip(ws, w_scrs):
            w_scr[...] = w_ref[...].astype(BF16)

    def matmul(cols):
        acc = jnp.dot(xs[0][...], w_scrs[0][:, cols], preferred_element_type=F32)
        for x_ref, w_scr in zip(xs[1:], w_scrs[1:]):
            acc = acc + jnp.dot(x_ref[...], w_scr[:, cols], preferred_element_type=F32)
        return acc

    tn = o_ref.shape[1]
    if has_res:
        o_ref[...] = (res_ref[...] + res_scale * matmul(slice(None))).astype(o_ref.dtype)
    elif head_dim:
        chunk = max(head_dim, MXU_WIDTH)
        for c0 in range(0, tn, chunk):
            acc = matmul(slice(c0, c0 + chunk))
            for h0 in range(0, chunk, head_dim):
                xh = acc[:, h0:h0 + head_dim]
                ms = jnp.mean(xh * xh, axis=-1, keepdims=True)
                yh = xh * lax.rsqrt(ms + RMS_EPS) * gain_ref[...]
                if has_rope:
                    yh = yh * cos_ref[...] + pltpu.roll(yh, head_dim // 2, 1) * sin_ref[...]
                o_ref[:, c0 + h0:c0 + h0 + head_dim] = yh.astype(o_ref.dtype)
    else:
        o_ref[...] = matmul(slice(None)).astype(o_ref.dtype)


def linear(pairs, *, n_cols, tm, tn, out_dtype=F32, res=None, res_scale=1.0,
           head_gain=None, rope=None, name="linear"):
    t = pairs[0][0].shape[0]
    in_specs, args, scratch = [], [], []
    for (x, xkb, w, wkb, wc0, kb) in pairs:
        in_specs.append(pl.BlockSpec((tm, kb), lambda j, i, xkb=xkb: (i, xkb)))
        args.append(x)
    for (x, xkb, w, wkb, wc0, kb) in pairs:
        in_specs.append(pl.BlockSpec((kb, tn), lambda j, i, wkb=wkb, wc0=wc0: (wkb, wc0 + j)))
        args.append(w)
        scratch.append(pltpu.VMEM((kb, tn), BF16))
    if res is not None:
        in_specs.append(pl.BlockSpec((tm, tn), lambda j, i: (i, j)))
        args.append(res)
    head_dim = 0
    if head_gain is not None:
        head_dim = head_gain.shape[0]
        in_specs.append(pl.BlockSpec((1, head_dim), lambda j, i: (0, 0)))
        args.append(head_gain.reshape(1, head_dim))
    if rope is not None:
        n_rope_blocks = rope[0].shape[0] // tm
        for tab in rope:
            in_specs.append(pl.BlockSpec((tm, head_dim), lambda j, i: (i % n_rope_blocks, 0)))
            args.append(tab)
    kernel = functools.partial(
        _linear_kernel, n_pairs=len(pairs), has_res=res is not None,
        res_scale=res_scale, head_dim=head_dim, has_rope=rope is not None)
    return pl.pallas_call(
        kernel,
        grid=(n_cols // tn, t // tm),
        in_specs=in_specs,
        out_specs=pl.BlockSpec((tm, tn), lambda j, i: (i, j)),
        out_shape=jax.ShapeDtypeStruct((t, n_cols), out_dtype),
        scratch_shapes=scratch,
        compiler_params=_params(2),
        name=name,
    )(*args)


def ffn_half_step(h, g_norm, w_gate, w_up, w_down, *, tm_up, tm_down):
    d_ff = w_gate.shape[1]
    x = rmsnorm_bf16(h, g_norm)
    act = swiglu_up(x, w_gate, w_up, tm=tm_up, tn=256)
    kb = d_ff // 2
    for half in range(2):
        h = linear([(act, half, w_down, half, 0, kb)], n_cols=D_MODEL, tm=tm_down,
                   tn=512, res=h, res_scale=0.5, name="ffn_down")
    return h


MOBA_HEADS_PER_STEP = 2


def _moba_prompt_kernel(q_ref, k_ref, v_ref, o_ref, kmean_scr, sel_scr, *, n_blocks):
    qi = pl.program_id(2)
    blk = MOBA_BLOCK
    dn_t = (((1,), (1,)), ((), ()))
    dn_tn = (((0,), (0,)), ((), ()))
    n_sel = min(MOBA_TOPK, n_blocks - 1)
    heads = range(MOBA_HEADS_PER_STEP)
    cols = [slice(h * A_HEAD_DIM, (h + 1) * A_HEAD_DIM) for h in heads]

    @pl.when(qi == 0)
    def _():
        for h in heads:
            kmean_scr[h] = jnp.mean(k_ref[:, cols[h]].reshape(n_blocks, blk, A_HEAD_DIM), axis=1)

    qs = []
    for h in heads:
        q = q_ref[:, cols[h]]
        gate = lax.dot_general(kmean_scr[h], q, dn_t, precision=HIGHEST,
                               preferred_element_type=F32)
        blk_idx = lax.broadcasted_iota(jnp.int32, gate.shape, 0)
        fully_past = blk_idx < qi
        gate = jnp.where(fully_past, gate, NEG_INF)
        rank = jnp.zeros(gate.shape, F32)
        for m in range(n_blocks):
            gm = gate[m:m + 1, :]
            beats = jnp.where(gm > gate, 1.0, jnp.where((gm == gate) & (blk_idx > m), 1.0, 0.0))
            rank = rank + beats
        sel_scr[h] = jnp.where(fully_past & (rank < n_sel), 1.0, 0.0)
        qs.append((q * (A_HEAD_DIM ** -0.5)).astype(BF16))

    def block_scores(h, start):
        kb = k_ref[pl.ds(start, blk), cols[h]].astype(BF16)
        vb = v_ref[pl.ds(start, blk), cols[h]].astype(BF16)
        return lax.dot_general(kb, qs[h], dn_t, preferred_element_type=F32), vb

    init = []
    for h in heads:
        s, vb = block_scores(h, pl.multiple_of(qi * blk, blk))
        key_idx = lax.broadcasted_iota(jnp.int32, s.shape, 0)
        qry_idx = lax.broadcasted_iota(jnp.int32, s.shape, 1)
        s = jnp.where(key_idx <= qry_idx, s, NEG_INF)
        m0 = jnp.max(s, axis=0, keepdims=True)
        p = jnp.exp(s - m0)
        l0 = jnp.sum(p, axis=0, keepdims=True)
        acc0 = lax.dot_general(vb, p.astype(BF16), dn_tn, preferred_element_type=F32)
        init.append((m0, l0, acc0))

    def past_block(n, carry):
        out = []
        for h in heads:
            m_run, l_run, acc = carry[h]
            s, vb = block_scores(h, pl.multiple_of(n * blk, blk))
            s = jnp.where(sel_scr[h, pl.ds(n, 1), :] > 0.5, s, NEG_INF)
            m_new = jnp.maximum(m_run, jnp.max(s, axis=0, keepdims=True))
            alpha = jnp.exp(m_run - m_new)
            p = jnp.exp(s - m_new)
            l_new = l_run * alpha + jnp.sum(p, axis=0, keepdims=True)
            acc = acc * alpha + lax.dot_general(vb, p.astype(BF16), dn_tn,
                                                preferred_element_type=F32)
            out.append((m_new, l_new, acc))
        return tuple(out)

    final = lax.fori_loop(0, qi, past_block, tuple(init))
    for h in heads:
        _, l_fin, acc = final[h]
        o_ref[:, cols[h]] = (acc / l_fin).T.astype(o_ref.dtype)


def moba_prompt(q, k, v, *, batch, seq):
    n_blocks = seq // MOBA_BLOCK
    width = MOBA_HEADS_PER_STEP * A_HEAD_DIM
    return pl.pallas_call(
        functools.partial(_moba_prompt_kernel, n_blocks=n_blocks),
        grid=(batch, A_HEADS // MOBA_HEADS_PER_STEP, n_blocks),
        in_specs=[pl.BlockSpec((MOBA_BLOCK, width), lambda b, h, i: (b * n_blocks + i, h)),
                  pl.BlockSpec((seq, width), lambda b, h, i: (b, h)),
                  pl.BlockSpec((seq, width), lambda b, h, i: (b, h))],
        out_specs=pl.BlockSpec((MOBA_BLOCK, width), lambda b, h, i: (b * n_blocks + i, h)),
        out_shape=jax.ShapeDtypeStruct((batch * seq, D_TOK), BF16),
        scratch_shapes=[pltpu.VMEM((MOBA_HEADS_PER_STEP, n_blocks, A_HEAD_DIM), F32),
                        pltpu.VMEM((MOBA_HEADS_PER_STEP, n_blocks, MOBA_BLOCK), F32)],
        compiler_params=_params(3),
        name="moba_prompt",
    )(q, k, v)


def _kmean_kernel(pt_ref, *refs):
    del pt_ref
    page_refs, o_ref = refs[:-1], refs[-1]
    total = jnp.sum(page_refs[0][0, 0], axis=0)
    for p_ref in page_refs[1:]:
        total = total + jnp.sum(p_ref[0, 0], axis=0)
    o_ref[0] = total * (1.0 / MOBA_BLOCK)


def cache_block_means(cache, page_table, *, layer, n_full):
    db = page_table.shape[0]

    def page_spec(p):
        return pl.BlockSpec(
            (1, 1, PAGE_SIZE, A_HEADS, A_HEAD_DIM),
            lambda b, n, pt, p=p: (layer, pt[b, PAGES_PER_BLOCK * n + p], 0, 0, 0))

    grid_spec = pltpu.PrefetchScalarGridSpec(
        num_scalar_prefetch=1,
        grid=(db, n_full),
        in_specs=[page_spec(p) for p in range(PAGES_PER_BLOCK)],
        out_specs=pl.BlockSpec((1, A_HEADS, A_HEAD_DIM), lambda b, n, pt: (b * n_full + n, 0, 0)),
    )
    out = pl.pallas_call(
        _kmean_kernel,
        grid_spec=grid_spec,
        out_shape=jax.ShapeDtypeStruct((db * n_full, A_HEADS, A_HEAD_DIM), F32),
        compiler_params=_params(2),
        name="cache_block_means",
    )(page_table, *([cache] * PAGES_PER_BLOCK))
    return out.reshape(db, n_full, D_TOK)


def _sample_topk_kernel(q_ref, km_ref, o_ref, *, n_sel):
    dn_t = (((1,), (1,)), ((), ()))
    q = q_ref[0]
    km = km_ref[0]
    for h in range(A_HEADS):
        cols = slice(h * A_HEAD_DIM, (h + 1) * A_HEAD_DIM)
        gate = lax.dot_general(q[:, cols], km[:, cols], dn_t, precision=HIGHEST,
                               preferred_element_type=F32)
        blk = lax.broadcasted_iota(jnp.int32, gate.shape, 1).astype(F32)
        lane = lax.broadcasted_iota(jnp.int32, (gate.shape[0], LANES), 1)
        tile = jnp.zeros((gate.shape[0], LANES), F32)
        for r in range(n_sel):
            best = jnp.max(gate, axis=1, keepdims=True)
            idx = jnp.min(jnp.where(gate == best, blk, float(gate.shape[1])),
                          axis=1, keepdims=True)
            tile = jnp.where(lane == r, idx, tile)
            gate = jnp.where(blk == idx, -jnp.inf, gate)
        o_ref[0, h] = tile.astype(jnp.int32)


def sample_topk(q, k_mean, *, n_sel):
    db, t, width = q.shape
    n_full = k_mean.shape[1]
    out = pl.pallas_call(
        functools.partial(_sample_topk_kernel, n_sel=n_sel),
        grid=(db,),
        in_specs=[pl.BlockSpec((1, t, width), lambda b: (b, 0, 0)),
                  pl.BlockSpec((1, n_full, width), lambda b: (b, 0, 0))],
        out_specs=pl.BlockSpec((1, A_HEADS, t, LANES), lambda b: (b, 0, 0, 0)),
        out_shape=jax.ShapeDtypeStruct((db, A_HEADS, t, LANES), jnp.int32),
        compiler_params=_params(1),
        name="sample_topk",
    )(q, k_mean)
    return out[..., :n_sel]


def _moba_sample_kernel(ph_ref, q_ref, kn_ref, vn_ref, ck_hbm, cv_hbm, o_ref, k_buf, v_buf, sem,
                        *, layer, t_len, n_pages):
    step = pl.program_id(0)
    n_steps = pl.num_programs(0)
    per_step = t_len * n_pages

    def page_copies(st, slot):
        head = st % A_HEADS
        copies = []
        for i in range(per_step):
            page = ph_ref[st * per_step + i]
            copies.append(pltpu.make_async_copy(
                ck_hbm.at[layer, page, :, head, :], k_buf.at[slot, i], sem.at[0, slot]))
            copies.append(pltpu.make_async_copy(
                cv_hbm.at[layer, page, :, head, :], v_buf.at[slot, i], sem.at[1, slot]))
        return copies

    @pl.when(step == 0)
    def _():
        for cp in page_copies(step, 0):
            cp.start()

    @pl.when(step + 1 < n_steps)
    def _():
        for cp in page_copies(step + 1, (step + 1) % 2):
            cp.start()

    slot = step % 2
    for cp in page_copies(step, slot):
        cp.wait()

    q = q_ref[0] * (A_HEAD_DIM ** -0.5)
    kn = kn_ref[0]
    vn = vn_ref[0]
    own_idx = lax.broadcasted_iota(jnp.int32, (t_len, 1), 0)
    for t in range(t_len):
        qt = q[t:t + 1, :]
        pages = range(t * n_pages, (t + 1) * n_pages)
        scores = [jnp.sum(k_buf[slot, i] * qt, axis=1, keepdims=True) for i in pages]
        s_own = jnp.sum(kn * qt, axis=1, keepdims=True)
        s_own = jnp.where(own_idx <= t, s_own, NEG_INF)
        m_col = scores[0]
        for s in scores[1:]:
            m_col = jnp.maximum(m_col, s)
        m = jnp.maximum(jnp.max(m_col, axis=0, keepdims=True),
                        jnp.max(s_own, axis=0, keepdims=True))
        p_own = jnp.exp(s_own - m)
        denom = jnp.sum(p_own, axis=0, keepdims=True)
        out = jnp.sum(p_own * vn, axis=0, keepdims=True)
        for s, i in zip(scores, pages):
            p = jnp.exp(s - m)
            denom = denom + jnp.sum(p, axis=0, keepdims=True)
            out = out + jnp.sum(p * v_buf[slot, i], axis=0, keepdims=True)
        o_ref[0, t:t + 1, :] = out / denom


def moba_sample(q, k_new, v_new, cache_k, cache_v, phys, *, layer):
    db, t_len, width = q.shape
    n_pages = phys.shape[0] // (db * A_HEADS * t_len)
    per_step = t_len * n_pages
    row_spec = pl.BlockSpec((1, t_len, A_HEAD_DIM), lambda s, ph: (s // A_HEADS, 0, s % A_HEADS))
    any_spec = pl.BlockSpec(memory_space=pl.ANY)
    page_buf = pltpu.VMEM((2, per_step, PAGE_SIZE, A_HEAD_DIM), F32)
    grid_spec = pltpu.PrefetchScalarGridSpec(
        num_scalar_prefetch=1,
        grid=(db * A_HEADS,),
        in_specs=[row_spec, row_spec, row_spec, any_spec, any_spec],
        out_specs=row_spec,
        scratch_shapes=[page_buf, page_buf, pltpu.SemaphoreType.DMA((2, 2))],
    )
    return pl.pallas_call(
        functools.partial(_moba_sample_kernel, layer=layer, t_len=t_len, n_pages=n_pages),
        grid_spec=grid_spec,
        out_shape=jax.ShapeDtypeStruct((db, t_len, width), F32),
        compiler_params=_params(1),
        name="moba_sample",
    )(phys, q, k_new, v_new, cache_k, cache_v)


def _mem_attend_kernel(mq_ref, k_ref, v_ref, g_ref, o_ref):
    dn_t = (((1,), (1,)), ((), ()))
    hd = MEM_HEAD_DIM
    for h in range(MEM_HEADS):
        cols = slice(h * hd, (h + 1) * hd)
        x = mq_ref[0, :, cols]
        ms = jnp.mean(x * x, axis=-1, keepdims=True)
        qn = (x * lax.rsqrt(ms + RMS_EPS) * g_ref[...]).astype(BF16)
        kh = k_ref[0, :, cols].astype(BF16)
        vh = v_ref[0, :, cols].astype(BF16)
        s = lax.dot_general(qn, kh, dn_t, preferred_element_type=F32) * (hd ** -0.5)
        s = s - jnp.max(s, axis=-1, keepdims=True)
        p = jnp.exp(s)
        denom = jnp.sum(p, axis=-1, keepdims=True)
        out = jnp.dot(p.astype(BF16), vh, preferred_element_type=F32) / denom
        o_ref[0, :, cols] = out.astype(o_ref.dtype)


def mem_attend(mq, g_q, mem_k, mem_v, *, tq, out_dtype):
    b, t, _ = mq.shape
    return pl.pallas_call(
        _mem_attend_kernel,
        grid=(b, t // tq),
        in_specs=[pl.BlockSpec((1, tq, D_MEM), lambda i, j: (i, j, 0)),
                  pl.BlockSpec((1, N_MEM, D_MEM), lambda i, j: (i, 0, 0)),
                  pl.BlockSpec((1, N_MEM, D_MEM), lambda i, j: (i, 0, 0)),
                  pl.BlockSpec((1, MEM_HEAD_DIM), lambda i, j: (0, 0))],
        out_specs=pl.BlockSpec((1, tq, D_MEM), lambda i, j: (i, j, 0)),
        out_shape=jax.ShapeDtypeStruct((b, t, D_MEM), out_dtype),
        compiler_params=_params(2),
        name="mem_attend",
    )(mq, mem_k, mem_v, g_q.reshape(1, MEM_HEAD_DIM))


def _head_ones(width):
    r = lax.broadcasted_iota(jnp.int32, (width, width), 0) // B_HEAD_DIM
    c = lax.broadcasted_iota(jnp.int32, (width, width), 1) // B_HEAD_DIM
    return jnp.where(r == c, 1.0, 0.0).astype(F32)


def _dot_f32(a, b):
    return jnp.dot(a, b, precision=HIGHEST, preferred_element_type=F32)


def _rwkv_prep_kernel(*refs, tm, t_len, halo):
    cur_refs, refs = refs[:4], refs[4:]
    if halo:
        halo_refs, shift_refs, refs = refs[:4], refs[4:8], refs[8:]
    else:
        prev_refs, refs = refs[:4], refs[4:]
    mu_refs, refs = refs[:4], refs[4:]
    (w0_ref, w2_ref, a0_ref, a2_ref, kk_ref, ka_ref, g2_ref,
     r_out, w_out, k_out, v_out, a_out, b_out, g_out) = refs
    at_seq_start = (pl.program_id(0) * tm) % t_len == 0

    def shifted(n):
        cur = cur_refs[n][...]
        if halo:
            first = jnp.where(at_seq_start, shift_refs[n][0], halo_refs[n][SUBLANES - 1:SUBLANES, :])
            row = lax.broadcasted_iota(jnp.int32, cur.shape, 0)
            prev = jnp.where(row == 0, first, pltpu.roll(cur, 1, 0))
        else:
            prev = prev_refs[n][...]
        return cur + (prev - cur) * mu_refs[n][...]

    r, k, v, lora = shifted(0), shifted(1), shifted(2), shifted(3)
    xw = lora[:, :DECAY_LORA]
    xa = lora[:, DECAY_LORA:DECAY_LORA + ICLR_LORA]
    xg = lora[:, DECAY_LORA + ICLR_LORA:]

    z = -(w0_ref[...] + _dot_f32(jnp.tanh(xw), w2_ref[...]))
    softplus = jnp.maximum(z, 0.0) + jnp.log(1.0 + jnp.exp(-jnp.abs(z)))
    decay = jnp.exp(-jnp.exp(-softplus - 0.5))
    a = jax.nn.sigmoid(a0_ref[...] + _dot_f32(xa, a2_ref[...]))

    kk = k * kk_ref[...]
    norm = jnp.sqrt(_dot_f32(kk * kk, _head_ones(kk.shape[1])))
    kk = kk / jnp.maximum(norm, 1e-12)

    r_out[...] = r
    w_out[...] = decay
    k_out[...] = k * (1.0 + (a - 1.0) * ka_ref[...])
    v_out[...] = v
    a_out[...] = -kk
    b_out[...] = kk * a
    g_out[...] = _dot_f32(jax.nn.sigmoid(xg), g2_ref[...])


def rwkv_prep(p_rkv, p_lora, shift_in, prm, *, batch, t_len, tm, tc):
    t = batch * t_len
    n_c = D_TOK // tc
    c3 = 3 * D_TOK
    row = lambda v: v.reshape(1, -1)
    halo = t_len % tm == 0
    mu = prm["mu"]

    def sect(s):
        return pl.BlockSpec((tm, tc), lambda i, j, s=s: (i, s * n_c + j))

    def vec(s=0):
        return pl.BlockSpec((1, tc), lambda i, j, s=s: (0, s * n_c + j))

    lora_tile = pl.BlockSpec((tm, LORA_COLS), lambda i, j: (i, 0))
    cur_specs = [sect(0), sect(1), sect(2), lora_tile]
    cur_args = [p_rkv, p_rkv, p_rkv, p_lora]
    if halo:
        rows_per_halo = tm // SUBLANES
        halo_row = lambda i: jnp.maximum(i * rows_per_halo - 1, 0)
        prev_specs = [pl.BlockSpec((SUBLANES, tc), lambda i, j, s=s: (halo_row(i), s * n_c + j))
                      for s in range(3)]
        prev_specs.append(pl.BlockSpec((SUBLANES, LORA_COLS), lambda i, j: (halo_row(i), 0)))
        seq = lambda i: (i * tm) // t_len
        prev_specs += [pl.BlockSpec((1, 1, tc), lambda i, j, s=s: (seq(i), 0, s * n_c + j))
                       for s in range(3)]
        prev_specs.append(pl.BlockSpec((1, 1, LORA_COLS), lambda i, j: (seq(i), 0, 0)))
        shift_rkv = shift_in[:, None, :c3]
        shift_lora = shift_in[:, None, c3:]
        prev_args = [p_rkv, p_rkv, p_rkv, p_lora, shift_rkv, shift_rkv, shift_rkv, shift_lora]
    else:
        assert tm % t_len == 0
        p_all = jnp.concatenate([p_rkv, p_lora], axis=1).reshape(batch, t_len, B_SHIFT_COLS)
        prev = jnp.concatenate([shift_in[:, None, :], p_all[:, :-1]], axis=1).reshape(t, B_SHIFT_COLS)
        prev_specs = list(cur_specs)
        prev_args = [prev[:, :c3]] * 3 + [prev[:, c3:]]
    param_specs = [vec(0), vec(1), vec(2), pl.BlockSpec((1, LORA_COLS), lambda i, j: (0, 0)),
                   vec(), pl.BlockSpec((DECAY_LORA, tc), lambda i, j: (0, j)),
                   vec(), pl.BlockSpec((ICLR_LORA, tc), lambda i, j: (0, j)),
                   vec(), vec(), pl.BlockSpec((GATE_LORA, tc), lambda i, j: (0, j))]
    param_args = [row(mu[:c3])] * 3 + [row(mu[c3:]), row(prm["w0"]), prm["w2"], row(prm["a0"]),
                                       prm["a2"], row(prm["k_k"]), row(prm["k_a"]), prm["g2"]]
    out_spec = pl.BlockSpec((tm, tc), lambda i, j: (i, j))
    n_out = 7
    return pl.pallas_call(
        functools.partial(_rwkv_prep_kernel, tm=tm, t_len=t_len, halo=halo),
        grid=(t // tm, n_c),
        in_specs=cur_specs + prev_specs + param_specs,
        out_specs=[out_spec] * n_out,
        out_shape=[jax.ShapeDtypeStruct((t, D_TOK), F32)] * n_out,
        compiler_params=_params(2),
        name="rwkv_prep",
    )(*cur_args, *prev_args, *param_args)


def _wkv_kernel(a_ref, w_ref, b_ref, k_ref, r_ref, v_ref, s0_ref, y_ref, s_out_ref, s_scr, *, tc):
    n_k = B_HEAD_DIM
    n_acc = 4

    @pl.when(pl.program_id(1) == 0)
    def _():
        s_scr[...] = s0_ref[0]

    def token(t, carry):
        sa_parts = [None] * n_acc
        for k in range(n_k):
            term = s_scr[k] * a_ref[0, t, pl.ds(k, 1), :]
            sa_parts[k % n_acc] = term if sa_parts[k % n_acc] is None else sa_parts[k % n_acc] + term
        sa = (sa_parts[0] + sa_parts[1]) + (sa_parts[2] + sa_parts[3])
        vt = v_ref[0, t]
        y_parts = [None] * n_acc
        for k in range(n_k):
            row = pl.ds(k, 1)
            s_new = (s_scr[k] * w_ref[0, t, row, :] + sa * b_ref[0, t, row, :]
                     + vt * k_ref[0, t, row, :])
            s_scr[k] = s_new
            term = s_new * r_ref[0, t, row, :]
            y_parts[k % n_acc] = term if y_parts[k % n_acc] is None else y_parts[k % n_acc] + term
        y_ref[0, t] = (y_parts[0] + y_parts[1]) + (y_parts[2] + y_parts[3])
        return carry

    lax.fori_loop(0, tc, token, 0)

    @pl.when(pl.program_id(1) == pl.num_programs(1) - 1)
    def _():
        s_out_ref[0] = s_scr[...]


def wkv_recurrence(a, w, b, k, r, v, s0, *, tc):
    g, t = a.shape[0], a.shape[1]
    op_spec = pl.BlockSpec((1, tc, B_HEAD_DIM, LANES), lambda i, j: (i, j, 0, 0))
    v_spec = pl.BlockSpec((1, tc, WKV_V_LO, LANES), lambda i, j: (i, j, 0, 0))
    s_spec = pl.BlockSpec((1, B_HEAD_DIM, WKV_V_LO, LANES), lambda i, j: (i, 0, 0, 0))
    return pl.pallas_call(
        functools.partial(_wkv_kernel, tc=tc),
        grid=(g, t // tc),
        in_specs=[op_spec] * 5 + [v_spec, s_spec],
        out_specs=[v_spec, s_spec],
        out_shape=[jax.ShapeDtypeStruct((g, t, WKV_V_LO, LANES), F32),
                   jax.ShapeDtypeStruct((g, B_HEAD_DIM, WKV_V_LO, LANES), F32)],
        scratch_shapes=[pltpu.VMEM((B_HEAD_DIM, WKV_V_LO, LANES), F32)],
        compiler_params=_params(2),
        name="wkv_recurrence",
    )(a, w, b, k, r, v, s0)


def _rwkv_post_kernel(y_ref, r_ref, k_ref, v_ref, g_ref, rk_ref, lw_ref, lb_ref, o_ref):
    y = y_ref[...]
    ones = _head_ones(y.shape[1])
    inv_n = 1.0 / B_HEAD_DIM
    mean = _dot_f32(y, ones) * inv_n
    d = y - mean
    var = _dot_f32(d * d, ones) * inv_n
    yn = d * lax.rsqrt(var + LNX_EPS) * lw_ref[...] + lb_ref[...]
    bonus = _dot_f32(r_ref[...] * k_ref[...] * rk_ref[...], ones) * v_ref[...]
    o_ref[...] = ((yn + bonus) * g_ref[...]).astype(o_ref.dtype)


def rwkv_post(y, r, k, v, gate, r_k, lnx_w, lnx_b, *, tm, tc, out_dtype):
    t = y.shape[0]
    row = lambda x: x.reshape(1, -1)
    tile = pl.BlockSpec((tm, tc), lambda i, j: (i, j))
    vec = pl.BlockSpec((1, tc), lambda i, j: (0, j))
    return pl.pallas_call(
        _rwkv_post_kernel,
        grid=(t // tm, D_TOK // tc),
        in_specs=[tile] * 5 + [vec] * 3,
        out_specs=tile,
        out_shape=jax.ShapeDtypeStruct((t, D_TOK), out_dtype),
        compiler_params=_params(2),
        name="rwkv_post",
    )(y, r, k, v, gate, row(r_k), row(lnx_w), row(lnx_b))


def _to_wkv_lanes(x, batch, t_len):
    n_chain = batch * B_HEADS
    g = n_chain // WKV_CHAINS_PER_GROUP
    x = x.reshape(batch, t_len, B_HEADS, B_HEAD_DIM).transpose(1, 3, 0, 2)
    x = x.reshape(t_len, B_HEAD_DIM, g, WKV_CHAINS_PER_GROUP).transpose(2, 0, 1, 3)
    return jnp.concatenate([x] * WKV_V_SPLIT, axis=-1)


def _v_to_wkv_lanes(x, batch, t_len):
    n_chain = batch * B_HEADS
    g = n_chain // WKV_CHAINS_PER_GROUP
    x = x.reshape(batch, t_len, B_HEADS, WKV_V_SPLIT, WKV_V_LO).transpose(1, 4, 3, 0, 2)
    x = x.reshape(t_len, WKV_V_LO, WKV_V_SPLIT, g, WKV_CHAINS_PER_GROUP).transpose(3, 0, 1, 2, 4)
    return x.reshape(g, t_len, WKV_V_LO, LANES)


def _v_from_wkv_lanes(y, batch, t_len):
    g = y.shape[0]
    y = y.reshape(g, t_len, WKV_V_LO, WKV_V_SPLIT, WKV_CHAINS_PER_GROUP).transpose(1, 3, 2, 0, 4)
    y = y.reshape(t_len, WKV_V_SPLIT, WKV_V_LO, batch, B_HEADS).transpose(3, 0, 4, 1, 2)
    return y.reshape(batch * t_len, D_TOK)


def _state_to_wkv_lanes(s, batch):
    g = batch * B_HEADS // WKV_CHAINS_PER_GROUP
    s = s.reshape(batch, B_HEADS, WKV_V_SPLIT, WKV_V_LO, B_HEAD_DIM).transpose(4, 3, 2, 0, 1)
    s = s.reshape(B_HEAD_DIM, WKV_V_LO, WKV_V_SPLIT, g, WKV_CHAINS_PER_GROUP).transpose(3, 0, 1, 2, 4)
    return s.reshape(g, B_HEAD_DIM, WKV_V_LO, LANES)


def _state_from_wkv_lanes(s, batch):
    g = s.shape[0]
    s = s.reshape(g, B_HEAD_DIM, WKV_V_LO, WKV_V_SPLIT, WKV_CHAINS_PER_GROUP).transpose(1, 3, 2, 0, 4)
    s = s.reshape(B_HEAD_DIM, WKV_V_SPLIT, WKV_V_LO, batch, B_HEADS).transpose(3, 4, 1, 2, 0)
    return s.reshape(batch, B_HEADS, B_HEAD_DIM, B_HEAD_DIM)


def rwkv_time_mix(p_rkv, p_lora, shift_in, state_in, prm, *, batch, t_len, tm, wkv_tc, out_dtype):
    r, w, k, v, a, b, gate = rwkv_prep(p_rkv, p_lora, shift_in, prm, batch=batch, t_len=t_len,
                                       tm=tm, tc=256)
    lanes = functools.partial(_to_wkv_lanes, batch=batch, t_len=t_len)
    y, s_out = wkv_recurrence(lanes(a), lanes(w), lanes(b), lanes(k), lanes(r),
                              _v_to_wkv_lanes(v, batch, t_len),
                              _state_to_wkv_lanes(state_in, batch), tc=wkv_tc)
    y = _v_from_wkv_lanes(y, batch, t_len)
    tok = rwkv_post(y, r, k, v, gate, prm["r_k"].reshape(-1), prm["lnx_w"], prm["lnx_b"],
                    tm=tm, tc=256, out_dtype=out_dtype)
    last = jnp.concatenate([p_rkv.reshape(batch, t_len, -1)[:, -1],
                            p_lora.reshape(batch, t_len, -1)[:, -1]], axis=1)
    return tok, _state_from_wkv_lanes(s_out, batch), last


def _rope_tables(pos):
    half = A_HEAD_DIM // 2
    inv_freq = jnp.power(jnp.float32(ROPE_THETA), -jnp.arange(half, dtype=F32) / half)
    ang = pos.astype(F32)[:, None] * inv_freq[None, :]
    cos, sin = jnp.cos(ang), jnp.sin(ang)
    return jnp.concatenate([cos, cos], axis=1), jnp.concatenate([-sin, sin], axis=1)


def kernel(x_prompt, x_sample, mem_prompt, cache_k, cache_v, page_table, cache_mem_k, cache_mem_v, state_wkv, state_shift, ffn1_norm, ffn1_w_gate, ffn1_w_up, ffn1_w_down, mix_norm, w_in_a, g_q_a, g_k_a, w_in_b, mu_b, w0_b, w2_b, a0_b, a2_b, g2_b, k_k_b, k_a_b, r_k_b, lnx_w_b, lnx_b_b, mem_norm, w_mem_k, w_mem_v, g_mem_q, g_mem_k, w_out, ffn2_norm, ffn2_w_gate, ffn2_w_up, ffn2_w_down):
    batch, seq, _ = x_prompt.shape
    dec_batch, dec_seq, _ = x_sample.shape
    depth = ffn1_norm.shape[0]
    n_pages = page_table.shape[1]
    past_len = n_pages * PAGE_SIZE
    n_full = past_len // MOBA_BLOCK
    assert past_len % MOBA_BLOCK == 0
    n_sel_s = min(MOBA_TOPK, n_full)
    tp, ts = batch * seq, dec_batch * dec_seq
    tm_p = 1024

    hp = x_prompt.reshape(tp, D_MODEL)
    hs = x_sample.reshape(ts, D_MODEL)
    rope_p = _rope_tables(jnp.arange(seq))
    rope_s = _rope_tables(jnp.tile(past_len + jnp.arange(dec_seq), dec_batch))

    outs = {name: [] for name in ("kp", "vp", "ks", "vs", "wkvp", "wkvs", "shp", "shs", "mkp", "mvp")}
    for i in range(depth):
        j = i // 2
        hp = ffn_half_step(hp, ffn1_norm[i], ffn1_w_gate[i], ffn1_w_up[i], ffn1_w_down[i],
                           tm_up=tm_p, tm_down=512)
        hs = ffn_half_step(hs, ffn1_norm[i], ffn1_w_gate[i], ffn1_w_up[i], ffn1_w_down[i],
                           tm_up=ts, tm_down=ts)

        mem_x = rmsnorm_bf16(mem_prompt.reshape(batch * N_MEM, D_MODEL), mem_norm[i])
        mem_kp = linear([(mem_x, 0, w_mem_k[i], 0, 0, D_MODEL)], n_cols=D_MEM, tm=batch * N_MEM,
                        tn=512, head_gain=g_mem_k[i], name="mem_k")
        mem_vp = linear([(mem_x, 0, w_mem_v[i], 0, 0, D_MODEL)], n_cols=D_MEM, tm=batch * N_MEM,
                        tn=512, name="mem_v")
        outs["mkp"].append(mem_kp.reshape(batch, N_MEM, MEM_HEADS, MEM_HEAD_DIM))
        outs["mvp"].append(mem_vp.reshape(batch, N_MEM, MEM_HEADS, MEM_HEAD_DIM))

        up = rmsnorm_bf16(hp, mix_norm[i])
        us = rmsnorm_bf16(hs, mix_norm[i])
        if i % 2 == 0:
            w_in = w_in_a[j]
            n_tok_blocks = D_TOK // 512

            def project(u, tm, rope, w_in=w_in):
                q = linear([(u, 0, w_in, 0, 0, D_MODEL)], n_cols=D_TOK, tm=tm, tn=512,
                           head_gain=g_q_a[j], rope=rope, name="in_a_q")
                k = linear([(u, 0, w_in, 0, n_tok_blocks, D_MODEL)], n_cols=D_TOK, tm=tm, tn=512,
                           head_gain=g_k_a[j], rope=rope, name="in_a_k")
                v = linear([(u, 0, w_in, 0, 2 * n_tok_blocks, D_MODEL)], n_cols=D_TOK, tm=tm,
                           tn=512, name="in_a_v")
                mq = linear([(u, 0, w_in, 0, 3 * n_tok_blocks, D_MODEL)], n_cols=D_MEM, tm=tm,
                            tn=512, name="in_a_mq")
                return q, k, v, mq

            qp, kp, vp, mqp = project(up, tm_p, rope_p)
            qs, ks, vs, mqs = project(us, ts, rope_s)
            tok_p = moba_prompt(qp, kp, vp, batch=batch, seq=seq)

            qs3 = qs.reshape(dec_batch, dec_seq, D_TOK)
            k_mean = cache_block_means(cache_k, page_table, layer=j, n_full=n_full)
            sel = sample_topk(qs3, k_mean, n_sel=n_sel_s)
            logical = sel[..., None] * PAGES_PER_BLOCK + jnp.arange(PAGES_PER_BLOCK)
            phys = jnp.take_along_axis(
                page_table[:, None, None, :],
                logical.reshape(dec_batch, A_HEADS, dec_seq, n_sel_s * PAGES_PER_BLOCK), axis=-1)
            tok_s = moba_sample(qs3, ks.reshape(dec_batch, dec_seq, D_TOK),
                                vs.reshape(dec_batch, dec_seq, D_TOK),
                                cache_k, cache_v, phys.reshape(-1).astype(jnp.int32), layer=j)
            tok_s = tok_s.reshape(ts, D_TOK).astype(BF16)
            outs["kp"].append(kp.reshape(batch, seq, A_HEADS, A_HEAD_DIM))
            outs["vp"].append(vp.reshape(batch, seq, A_HEADS, A_HEAD_DIM))
            outs["ks"].append(ks.reshape(dec_batch, dec_seq, A_HEADS, A_HEAD_DIM))
            outs["vs"].append(vs.reshape(dec_batch, dec_seq, A_HEADS, A_HEAD_DIM))
        else:
            c3 = 3 * D_TOK
            w_in = w_in_b[j]
            w_lora = w_in[:, c3:B_SHIFT_COLS]
            w_mq = w_in[:, B_SHIFT_COLS:]
            prm = dict(mu=mu_b[j], w0=w0_b[j], w2=w2_b[j], a0=a0_b[j], a2=a2_b[j], g2=g2_b[j],
                       k_k=k_k_b[j], k_a=k_a_b[j], r_k=r_k_b[j], lnx_w=lnx_w_b[j], lnx_b=lnx_b_b[j])

            def project(u, tm, w_in=w_in, w_lora=w_lora, w_mq=w_mq):
                p_rkv = linear([(u, 0, w_in, 0, 0, D_MODEL)], n_cols=c3, tm=tm, tn=512, name="in_b_rkv")
                p_lora = linear([(u, 0, w_lora, 0, 0, D_MODEL)], n_cols=LORA_COLS, tm=tm,
                                tn=LORA_COLS, name="in_b_lora")
                mq = linear([(u, 0, w_mq, 0, 0, D_MODEL)], n_cols=D_MEM, tm=tm, tn=512, name="in_b_mq")
                return p_rkv, p_lora, mq

            rkv_p, lora_p, mqp = project(up, tm_p)
            rkv_s, lora_s, mqs = project(us, ts)
            tok_p, st_p, last_p = rwkv_time_mix(
                rkv_p, lora_p, jnp.zeros((batch, B_SHIFT_COLS), F32),
                jnp.zeros((batch, B_HEADS, B_HEAD_DIM, B_HEAD_DIM), F32), prm,
                batch=batch, t_len=seq, tm=256, wkv_tc=32, out_dtype=BF16)
            tok_s, st_s, last_s = rwkv_time_mix(
                rkv_s, lora_s, state_shift[j], state_wkv[j], prm,
                batch=dec_batch, t_len=dec_seq, tm=ts, wkv_tc=dec_seq, out_dtype=BF16)
            outs["wkvp"].append(st_p)
            outs["wkvs"].append(st_s)
            outs["shp"].append(last_p)
            outs["shs"].append(last_s)

        mem_p = mem_attend(mqp.reshape(batch, seq, D_MEM), g_mem_q[i],
                           mem_kp.reshape(batch, N_MEM, D_MEM), mem_vp.reshape(batch, N_MEM, D_MEM),
                           tq=512, out_dtype=BF16).reshape(tp, D_MEM)
        mem_s = mem_attend(mqs.reshape(dec_batch, dec_seq, D_MEM), g_mem_q[i],
                           cache_mem_k[i].reshape(dec_batch, N_MEM, D_MEM),
                           cache_mem_v[i].reshape(dec_batch, N_MEM, D_MEM),
                           tq=dec_seq, out_dtype=F32).reshape(ts, D_MEM).astype(BF16)
        mem_row_block = D_TOK // D_MEM
        hp = linear([(tok_p, 0, w_out[i], 0, 0, D_TOK), (mem_p, 0, w_out[i], mem_row_block, 0, D_MEM)],
                    n_cols=D_MODEL, tm=tm_p, tn=512, res=hp, name="out_proj")
        hs = linear([(tok_s, 0, w_out[i], 0, 0, D_TOK), (mem_s, 0, w_out[i], mem_row_block, 0, D_MEM)],
                    n_cols=D_MODEL, tm=ts, tn=512, res=hs, name="out_proj")

        hp = ffn_half_step(hp, ffn2_norm[i], ffn2_w_gate[i], ffn2_w_up[i], ffn2_w_down[i],
                           tm_up=tm_p, tm_down=512)
        hs = ffn_half_step(hs, ffn2_norm[i], ffn2_w_gate[i], ffn2_w_up[i], ffn2_w_down[i],
                           tm_up=ts, tm_down=ts)

    st = lambda name: jnp.stack(outs[name])
    return (hp.reshape(batch, seq, D_MODEL), hs.reshape(dec_batch, dec_seq, D_MODEL),
            st("kp"), st("vp"), st("ks"), st("vs"), st("wkvp"), st("wkvs"),
            st("shp"), st("shs"), st("mkp"), st("mvp"))
```
